```python
import jax, jax.numpy as jnp
from jax import lax
import numpy as np

D_MODEL = 1024
BATCH = 2
SEQ = 8192
DEPTH = 1

D_MIX = D_MODEL
RET_HEADS = 4
RET_DK = 128
RET_DV = 128
RET_CHUNK = 128
NSA_HEADS = 8
NSA_KV_GROUPS = 2
NSA_HPG = NSA_HEADS // NSA_KV_GROUPS
NSA_DH = 64
CMP_LEN = 32
CMP_STRIDE = 16
CMP_HID = 256
SEL_BLOCK = 64
SEL_TOPN = 16
WINDOW = 512
Q_BLOCK = 128
FORCED_SCORE = 1e9
ROPE_THETA = 10000.0
PEER_HEADS = 8
PEER_NKEYS = 128
PEER_EXPERTS = PEER_NKEYS * PEER_NKEYS
PEER_TOPK = 16
PEER_DQ = 256
PEER_DHALF = PEER_DQ // 2
PEER_TOK_BLOCK = 128
NORM_EPS = 1e-6
KV_COLS = NSA_KV_GROUPS * NSA_DH
IN_SPLITS = (RET_HEADS * RET_DK, RET_HEADS * RET_DK, RET_HEADS * RET_DV, RET_HEADS * RET_DV,
             NSA_HEADS * NSA_DH, KV_COLS, KV_COLS, KV_COLS, KV_COLS, KV_COLS, KV_COLS,
             NSA_HEADS * 3)
IN_COLS = sum(IN_SPLITS)

kernel_name = "hymba_retention_nsa_peer_block"


def _rms_norm(x, g):
    xf = x.astype(jnp.float32)
    y = xf * lax.rsqrt(jnp.mean(xf * xf, axis=-1, keepdims=True) + NORM_EPS)
    return (y * g.astype(jnp.float32)).astype(x.dtype)


def _modulate(h, shift, scale):
    return h * (1.0 + scale[:, None, :]) + shift[:, None, :]


def _rope(x, pos):
    d = x.shape[-1]
    inv = ROPE_THETA ** (-jnp.arange(0, d, 2, dtype=jnp.float32) / d)
    ang = pos.astype(jnp.float32)[:, None] * inv[None, :]
    cos = jnp.cos(ang)[None, :, None, :].astype(x.dtype)
    sin = jnp.sin(ang)[None, :, None, :].astype(x.dtype)
    x1, x2 = jnp.split(x, 2, axis=-1)
    return jnp.concatenate([x1 * cos - x2 * sin, x2 * cos + x1 * sin], axis=-1)


def _masked_softmax(s, mask):
    s = jnp.where(mask, s, -jnp.inf)
    m = jnp.max(s, axis=-1, keepdims=True)
    m = jnp.where(jnp.isfinite(m), m, 0.0)
    e = jnp.where(mask, jnp.exp(s - m), 0.0)
    return e / jnp.maximum(jnp.sum(e, axis=-1, keepdims=True), jnp.finfo(jnp.float32).tiny)


def _retention(q, k, v, g):
    B, S, H, DK = q.shape
    DV = v.shape[-1]
    C = RET_CHUNK
    nc = S // C
    f32 = jnp.float32
    log_g = jnp.log1p(-jnp.exp2(-5.0 - jnp.arange(H, dtype=f32)))
    n = jnp.arange(C, dtype=f32)
    diff = n[:, None] - n[None, :]
    causal = diff >= 0
    dmat = jnp.where(causal[None], jnp.exp(jnp.where(causal, diff, 0.0)[None] * log_g[:, None, None]), 0.0)
    qc = q.astype(f32).reshape(B, nc, C, H, DK)
    kc = (k.astype(f32) * DK ** -0.5).reshape(B, nc, C, H, DK)
    vc = v.astype(f32).reshape(B, nc, C, H, DV)
    scores = jnp.einsum('bcnhk,bcmhk->bchnm', qc, kc) * dmat[None, None]
    inner = jnp.einsum('bchnm,bcmhv->bcnhv', scores, vc)
    zeta = jnp.exp((C - 1 - n)[None, :] * log_g[:, None])
    kv = jnp.einsum('bcmhk,bcmhv,hm->cbhkv', kc, vc, zeta)
    cdecay = jnp.exp(C * log_g)[None, :, None, None]

    def step(state, kv_c):
        return cdecay * state + kv_c, state

    _, s_prev = lax.scan(step, jnp.zeros((B, H, DK, DV), f32), kv)
    xi = jnp.exp((n + 1.0)[None, :] * log_g[:, None])
    cross = jnp.einsum('bcnhk,cbhkv,hn->bcnhv', qc, s_prev, xi)
    y = (inner + cross).reshape(B, S, H, DV)
    mu = jnp.mean(y, axis=-1, keepdims=True)
    var = jnp.mean((y - mu) ** 2, axis=-1, keepdims=True)
    y = (y - mu) * lax.rsqrt(var + NORM_EPS)
    y = jax.nn.silu(g.astype(f32)) * y
    return y.reshape(B, S, H * DV).astype(q.dtype)


def _compress(kraw, pe, w1, b1, w2):
    B, S, G, DH = kraw.shape
    ncb = (S - CMP_LEN) // CMP_STRIDE + 1
    idx = jnp.arange(ncb)[:, None] * CMP_STRIDE + jnp.arange(CMP_LEN)[None, :]
    blk = kraw[:, idx] + pe[None, None, :, None, :]
    blk = blk.transpose(0, 1, 3, 2, 4).reshape(B, ncb, G, CMP_LEN * DH)
    return jax.nn.gelu(blk @ w1 + b1) @ w2


def _nsa(q, k_c, v_c, k_s, v_s, k_w, v_w, gate_logits, pe_k, pe_v, wk1, bk1, wk2, wv1, bv1, wv2):
    B, S = q.shape[:2]
    G, HPG, DH = NSA_KV_GROUPS, NSA_HPG, NSA_DH
    f32 = jnp.float32
    kc = _compress(k_c, pe_k, wk1, bk1, wk2)
    vc = _compress(v_c, pe_v, wv1, bv1, wv2)
    ncb = kc.shape[1]
    nsel = S // SEL_BLOCK
    n_top = min(SEL_TOPN, nsel)
    nqb = S // Q_BLOCK
    cmp_start = jnp.arange(ncb) * CMP_STRIDE
    cmp_end = cmp_start + CMP_LEN - 1
    sel_start = jnp.arange(nsel) * SEL_BLOCK
    overlap = jnp.clip(jnp.minimum(cmp_start[:, None] + CMP_LEN, sel_start[None, :] + SEL_BLOCK)
                       - jnp.maximum(cmp_start[:, None], sel_start[None, :]), 0).astype(f32) / CMP_STRIDE
    ks_g = k_s.reshape(B, nsel, SEL_BLOCK, G, DH).transpose(0, 3, 1, 2, 4)
    vs_g = v_s.reshape(B, nsel, SEL_BLOCK, G, DH).transpose(0, 3, 1, 2, 4)
    kw_pad = jnp.pad(k_w, ((0, 0), (WINDOW, 0), (0, 0), (0, 0)))
    vw_pad = jnp.pad(v_w, ((0, 0), (WINDOW, 0), (0, 0), (0, 0)))
    qg = q.reshape(B, nqb, Q_BLOCK, G, HPG, DH).swapaxes(0, 1)
    gg = gate_logits.reshape(B, nqb, Q_BLOCK, G, HPG, 3).swapaxes(0, 1)
    scale = DH ** -0.5
    gather = jax.vmap(jax.vmap(lambda tbl, ix: tbl[ix]))
    blk_ids = jnp.arange(nsel)

    def body(args):
        qb, qblk, gblk = args
        t = qb * Q_BLOCK + jnp.arange(Q_BLOCK)
        s_c = jnp.einsum('bqghd,bngd->bghqn', qblk, kc).astype(f32) * scale
        p_c = _masked_softmax(s_c, cmp_end[None, :] <= t[:, None])
        o_c = jnp.einsum('bghqn,bngd->bqghd', p_c.astype(vc.dtype), vc)
        imp = jnp.einsum('bghqn,nj->bgqj', p_c, overlap)
        cur = t // SEL_BLOCK
        forced = (blk_ids[None, :] == 0) | (blk_ids[None, :] == cur[:, None]) | (blk_ids[None, :] == cur[:, None] - 1)
        imp = jnp.where(forced, FORCED_SCORE, imp)
        imp = jnp.where(blk_ids[None, :] <= cur[:, None], imp, -jnp.inf)
        top_v, top_i = lax.top_k(imp, n_top)
        ksel = gather(ks_g, top_i)
        vsel = gather(vs_g, top_i)
        pos = top_i[..., None] * SEL_BLOCK + jnp.arange(SEL_BLOCK)
        m_s = jnp.isfinite(top_v)[..., None] & (pos <= t[None, None, :, None, None])
        s_s = jnp.einsum('bqghd,bgqnkd->bghqnk', qblk, ksel).astype(f32) * scale
        p_s = _masked_softmax(s_s.reshape(B, G, HPG, Q_BLOCK, n_top * SEL_BLOCK),
                              m_s.reshape(B, G, 1, Q_BLOCK, n_top * SEL_BLOCK))
        o_s = jnp.einsum('bghqm,bgqmd->bqghd', p_s.astype(vsel.dtype),
                         vsel.reshape(B, G, Q_BLOCK, n_top * SEL_BLOCK, DH))
        kwb = lax.dynamic_slice_in_dim(kw_pad, qb * Q_BLOCK, Q_BLOCK + WINDOW, axis=1)
        vwb = lax.dynamic_slice_in_dim(vw_pad, qb * Q_BLOCK, Q_BLOCK + WINDOW, axis=1)
        sp = qb * Q_BLOCK - WINDOW + jnp.arange(Q_BLOCK + WINDOW)
        m_w = (sp[None, :] >= 0) & (sp[None, :] <= t[:, None]) & (sp[None, :] > t[:, None] - WINDOW)
        s_w = jnp.einsum('bqghd,bkgd->bghqk', qblk, kwb).astype(f32) * scale
        p_w = _masked_softmax(s_w, m_w)
        o_w = jnp.einsum('bghqk,bkgd->bqghd', p_w.astype(vwb.dtype), vwb)
        gate = jax.nn.sigmoid(gblk.astype(f32))
        out = gate[..., 0:1] * o_c + gate[..., 1:2] * o_s + gate[..., 2:3] * o_w
        return out.astype(qblk.dtype)

    out = lax.map(body, (jnp.arange(nqb), qg, gg))
    return out.swapaxes(0, 1).reshape(B, S, NSA_HEADS * DH)


def _peer(h, w_q, subkeys, w_down, w_up):
    B, S, D = h.shape
    TB = PEER_TOK_BLOCK
    nb = S // TB
    hb = h.reshape(B, nb, TB, D).swapaxes(0, 1)

    def body(hblk):
        q = (hblk @ w_q).reshape(B, TB, PEER_HEADS, 2, PEER_DHALF)
        s = jnp.einsum('btphd,phnd->btphn', q, subkeys).astype(jnp.float32)
        v, i = lax.top_k(s, PEER_TOPK)
        cand = (v[..., 0, :, None] + v[..., 1, None, :]).reshape(B, TB, PEER_HEADS, PEER_TOPK * PEER_TOPK)
        cidx = (i[..., 0, :, None] * PEER_NKEYS + i[..., 1, None, :]).reshape(B, TB, PEER_HEADS, PEER_TOPK * PEER_TOPK)
        tv, tj = lax.top_k(cand, PEER_TOPK)
        eidx = jnp.take_along_axis(cidx, tj, axis=-1)
        w = jax.nn.softmax(tv, axis=-1)
        u = w_down[eidx]
        a = jax.nn.gelu(jnp.einsum('btd,btpkd->btpk', hblk, u).astype(jnp.float32))
        vv = w_up[eidx]
        return jnp.einsum('btpk,btpkd->btd', (w * a).astype(hblk.dtype), vv)

    out = lax.map(body, hb)
    return out.swapaxes(0, 1).reshape(B, S, D)


def setup_inputs(seed: int = 0) -> dict:
    key = jax.random.key(seed)
    ks = jax.random.split(key, 24)
    L = DEPTH

    def nrm(k, shape, s):
        return jax.random.normal(k, shape, jnp.float32) * s

    return {
        "x": nrm(ks[0], (BATCH, SEQ, D_MODEL), 1.0),
        "c": nrm(ks[1], (BATCH, D_MODEL), 1.0),
        "w_ada": nrm(ks[2], (L, D_MODEL, 6 * D_MODEL), 0.5 * D_MODEL ** -0.5),
        "b_ada": nrm(ks[3], (L, 6 * D_MODEL), 0.02),
        "g_norm_mix": 1.0 + nrm(ks[4], (L, D_MODEL), 0.02),
        "w_in": nrm(ks[5], (L, D_MODEL, IN_COLS), D_MODEL ** -0.5),
        "pe_cmp_k": nrm(ks[6], (L, CMP_LEN, NSA_DH), 0.1),
        "pe_cmp_v": nrm(ks[7], (L, CMP_LEN, NSA_DH), 0.1),
        "w_cmp_k1": nrm(ks[8], (L, CMP_LEN * NSA_DH, CMP_HID), (CMP_LEN * NSA_DH) ** -0.5),
        "b_cmp_k1": nrm(ks[9], (L, CMP_HID), 0.02),
        "w_cmp_k2": nrm(ks[10], (L, CMP_HID, NSA_DH), CMP_HID ** -0.5),
        "w_cmp_v1": nrm(ks[11], (L, CMP_LEN * NSA_DH, CMP_HID), (CMP_LEN * NSA_DH) ** -0.5),
        "b_cmp_v1": nrm(ks[12], (L, CMP_HID), 0.02),
        "w_cmp_v2": nrm(ks[13], (L, CMP_HID, NSA_DH), CMP_HID ** -0.5),
        "w_out": nrm(ks[14], (L, D_MIX, D_MODEL), D_MIX ** -0.5),
        "g_norm_ffn": 1.0 + nrm(ks[15], (L, D_MODEL), 0.02),
        "w_peer_q": nrm(ks[16], (L, D_MODEL, PEER_HEADS * PEER_DQ), D_MODEL ** -0.5),
        "peer_subkeys": nrm(ks[17], (L, PEER_HEADS, 2, PEER_NKEYS, PEER_DHALF), PEER_DHALF ** -0.5),
        "peer_down": nrm(ks[18], (L, PEER_EXPERTS, D_MODEL), D_MODEL ** -0.5),
        "peer_up": nrm(ks[19], (L, PEER_EXPERTS, D_MODEL), PEER_HEADS ** -0.5),
        "g_norm_final": 1.0 + nrm(ks[20], (D_MODEL,), 0.02),
    }


def reference(x, c, w_ada, b_ada, g_norm_mix, w_in, pe_cmp_k, pe_cmp_v, w_cmp_k1, b_cmp_k1, w_cmp_k2,
              w_cmp_v1, b_cmp_v1, w_cmp_v2, w_out, g_norm_ffn, w_peer_q, peer_subkeys, peer_down, peer_up,
              g_norm_final):
    B, S, D = x.shape
    pos = jnp.arange(S)
    split_points = [int(v) for v in np.cumsum(IN_SPLITS)[:-1]]
    for l in range(DEPTH):
        mod = jax.nn.silu(c) @ w_ada[l] + b_ada[l]
        sh1, sc1, ga1, sh2, sc2, ga2 = jnp.split(mod, 6, axis=-1)
        h = _modulate(_rms_norm(x, g_norm_mix[l]), sh1, sc1)
        proj = h @ w_in[l]
        (r_q, r_k, r_v, r_g, n_q, n_kc, n_vc, n_ks, n_vs, n_kw, n_vw, n_gate) = jnp.split(proj, split_points, axis=-1)
        r_q = _rope(r_q.reshape(B, S, RET_HEADS, RET_DK), pos)
        r_k = _rope(r_k.reshape(B, S, RET_HEADS, RET_DK), pos)
        ret_out = _retention(r_q, r_k, r_v.reshape(B, S, RET_HEADS, RET_DV), r_g.reshape(B, S, RET_HEADS, RET_DV))
        kvshape = (B, S, NSA_KV_GROUPS, NSA_DH)
        nsa_out = _nsa(_rope(n_q.reshape(B, S, NSA_HEADS, NSA_DH), pos),
                       _rope(n_kc.reshape(kvshape), pos), n_vc.reshape(kvshape),
                       _rope(n_ks.reshape(kvshape), pos), n_vs.reshape(kvshape),
                       _rope(n_kw.reshape(kvshape), pos), n_vw.reshape(kvshape),
                       n_gate, pe_cmp_k[l], pe_cmp_v[l], w_cmp_k1[l], b_cmp_k1[l], w_cmp_k2[l],
                       w_cmp_v1[l], b_cmp_v1[l], w_cmp_v2[l])
        mix = jnp.concatenate([ret_out, nsa_out], axis=-1) @ w_out[l]
        x = x + ga1[:, None, :] * mix
        h = _modulate(_rms_norm(x, g_norm_ffn[l]), sh2, sc2)
        x = x + ga2[:, None, :] * _peer(h, w_peer_q[l], peer_subkeys[l], peer_down[l], peer_up[l])
    return _rms_norm(x, g_norm_final)
```

```python
import functools
import math

import jax
import jax.numpy as jnp
from jax import lax
from jax.experimental import pallas as pl
from jax.experimental.pallas import tpu as pltpu

F32 = jnp.float32
BF16 = jnp.bfloat16

RET_HEADS = 4
RET_DK = 128
RET_DV = 128
RET_CHUNK = 128
NSA_HEADS = 8
NSA_KV_GROUPS = 2
NSA_HPG = NSA_HEADS // NSA_KV_GROUPS
NSA_DH = 64
CMP_LEN = 32
CMP_STRIDE = 16
CMP_HID = 256
SEL_BLOCK = 64
SEL_TOPN = 16
WINDOW = 512
FORCED_SCORE = 1e9
ROPE_THETA = 10000.0
PEER_HEADS = 8
PEER_TOPK = 16
NORM_EPS = 1e-6

LANES = 128
MASK_NEG = -1e30
VMEM_LIMIT = 56 * 1024 * 1024


def _cparams(sem):
    return pltpu.CompilerParams(dimension_semantics=sem, vmem_limit_bytes=VMEM_LIMIT)


def _dot(a, b):
    return jnp.dot(a, b, preferred_element_type=F32)


def _dot_nt(a, b):
    return lax.dot_general(a, b, (((1,), (1,)), ((), ())), preferred_element_type=F32)


def _dot_tn(a, b):
    return lax.dot_general(a, b, (((0,), (0,)), ((), ())), preferred_element_type=F32)


def _split_bf16(a):
    hi = a.astype(BF16)
    lo = (a - hi.astype(F32)).astype(BF16)
    return hi, lo


def _ada_kernel(c_ref, w_ref, b_ref, o_ref):
    c = c_ref[...]
    a = c * jax.nn.sigmoid(c)
    a_hi, a_lo = _split_bf16(a)
    w_hi, w_lo = _split_bf16(w_ref[...])
    o_ref[...] = _dot(a_hi, w_hi) + _dot(a_hi, w_lo) + _dot(a_lo, w_hi) + b_ref[...]


def _ada(c, w, b):
    bsz, d = c.shape
    n = w.shape[1]
    rows = 16
    cp = jnp.zeros((rows, d), F32).at[:bsz].set(c)
    tn = 1024
    out = pl.pallas_call(
        _ada_kernel,
        grid=(n // tn,),
        in_specs=[pl.BlockSpec((rows, d), lambda j: (0, 0)),
                  pl.BlockSpec((d, tn), lambda j: (0, j)),
                  pl.BlockSpec((1, tn), lambda j: (0, j))],
        out_specs=pl.BlockSpec((rows, tn), lambda j: (0, j)),
        out_shape=jax.ShapeDtypeStruct((rows, n), F32),
        compiler_params=_cparams(("parallel",)),
        name="ada",
    )(cp, w, b.reshape(1, n))
    return out[:bsz]


_ROPE_SEGS = (("rq", RET_HEADS * RET_DK, RET_DK, 1.0),
              ("rk", RET_HEADS * RET_DK, RET_DK, RET_DK ** -0.5),
              ("nq", NSA_HEADS * NSA_DH, NSA_DH, NSA_DH ** -0.5),
              ("kc", LANES, NSA_DH, 1.0),
              ("ks", LANES, NSA_DH, 1.0),
              ("kw", LANES, NSA_DH, 1.0))
_PLAIN_SEGS = (("rv", RET_HEADS * RET_DV, BF16),
               ("rg", RET_HEADS * RET_DV, F32),
               ("vc", LANES, BF16),
               ("vs", LANES, BF16),
               ("vw", LANES, BF16),
               ("gate", LANES, F32))


def _norm_mod(x, g, sc, sh):
    y = x * lax.rsqrt(jnp.mean(x * x, axis=-1, keepdims=True) + NORM_EPS)
    return (y * g) * (1.0 + sc) + sh


def _inproj_kernel(x_ref, sc_ref, sh_ref, g_ref, w_ref, wrot_ref, wpl_ref,
                   c128_ref, s128_ref, c64_ref, s64_ref, *out_refs):
    h = _norm_mod(x_ref[0], g_ref[...], sc_ref[0], sh_ref[0])
    hb = h.astype(BF16)
    outs = list(out_refs)
    off = 0
    for (_, width, hd, scale) in _ROPE_SEGS:
        o_ref = outs.pop(0)
        p = _dot(hb, w_ref[:, off:off + width])
        pr = _dot(hb, wrot_ref[:, off:off + width])
        cos = c128_ref[...] if hd == LANES else c64_ref[...]
        sin = s128_ref[...] if hd == LANES else s64_ref[...]
        for cb in range(width // LANES):
            sl = slice(cb * LANES, (cb + 1) * LANES)
            o = p[:, sl] * cos + pr[:, sl] * sin
            if scale != 1.0:
                o = o * scale
            o_ref[0, :, sl] = o.astype(o_ref.dtype)
        off += width
    off = 0
    for (_, width, _) in _PLAIN_SEGS:
        o_ref = outs.pop(0)
        o_ref[0] = _dot(hb, wpl_ref[:, off:off + width]).astype(o_ref.dtype)
        off += width


def _rot_cols(wseg, hd):
    d = wseg.shape[0]
    w3 = wseg.reshape(d, -1, hd)
    half = hd // 2
    return jnp.concatenate([-w3[..., half:], w3[..., :half]], axis=-1).reshape(d, -1)


def _rope_tables(seq, hd):
    pos = jnp.arange(seq)
    inv = ROPE_THETA ** (-jnp.arange(0, hd, 2, dtype=F32) / hd)
    ang = pos.astype(F32)[:, None] * inv[None, :]
    reps = 2 * LANES // hd
    return jnp.tile(jnp.cos(ang), (1, reps)), jnp.tile(jnp.sin(ang), (1, reps))


def _inproj(x, sc, sh, g, w_in, tm=512):
    bsz, seq, d = x.shape
    splits = (RET_HEADS * RET_DK, RET_HEADS * RET_DK, RET_HEADS * RET_DV, RET_HEADS * RET_DV,
              NSA_HEADS * NSA_DH) + (NSA_KV_GROUPS * NSA_DH,) * 6 + (NSA_HEADS * 3,)
    pts = []
    acc = 0
    for s in splits[:-1]:
        acc += s
        pts.append(acc)
    (r_q, r_k, r_v, r_g, n_q, n_kc, n_vc, n_ks, n_vs, n_kw, n_vw, n_gate) = jnp.split(w_in, pts, axis=-1)
    rope_w = {"rq": r_q, "rk": r_k, "nq": n_q, "kc": n_kc, "ks": n_ks, "kw": n_kw}
    w_rope = jnp.concatenate([rope_w[n] for (n, _, _, _) in _ROPE_SEGS], axis=-1).astype(BF16)
    w_rot = jnp.concatenate([_rot_cols(rope_w[n], hd) for (n, _, hd, _) in _ROPE_SEGS], axis=-1).astype(BF16)
    gate_pad = jnp.zeros((d, LANES), F32).at[:, :NSA_HEADS * 3].set(n_gate)
    w_plain = jnp.concatenate([r_v, r_g, n_vc, n_vs, n_vw, gate_pad], axis=-1).astype(BF16)
    c128, s128 = _rope_tables(seq, RET_DK)
    c64, s64 = _rope_tables(seq, NSA_DH)

    tm = min(tm, seq)
    nrope = w_rope.shape[1]
    npl = w_plain.shape[1]
    const2 = lambda b, i: (0, 0)
    tab = pl.BlockSpec((tm, LANES), lambda b, i: (i, 0))
    in_specs = [pl.BlockSpec((1, tm, d), lambda b, i: (b, i, 0)),
                pl.BlockSpec((1, 1, d), lambda b, i: (b, 0, 0)),
                pl.BlockSpec((1, 1, d), lambda b, i: (b, 0, 0)),
                pl.BlockSpec((1, d), const2),
                pl.BlockSpec((d, nrope), const2, pipeline_mode=pl.Buffered(1)),
                pl.BlockSpec((d, nrope), const2, pipeline_mode=pl.Buffered(1)),
                pl.BlockSpec((d, npl), const2, pipeline_mode=pl.Buffered(1)),
                tab, tab, tab, tab]
    out_specs = []
    out_shape = []
    for (_, width, _, _) in _ROPE_SEGS:
        out_specs.append(pl.BlockSpec((1, tm, width), lambda b, i: (b, i, 0)))
        out_shape.append(jax.ShapeDtypeStruct((bsz, seq, width), BF16))
    for (_, width, dt) in _PLAIN_SEGS:
        out_specs.append(pl.BlockSpec((1, tm, width), lambda b, i: (b, i, 0)))
        out_shape.append(jax.ShapeDtypeStruct((bsz, seq, width), dt))
    outs = pl.pallas_call(
        _inproj_kernel,
        grid=(bsz, seq // tm),
        in_specs=in_specs,
        out_specs=out_specs,
        out_shape=out_shape,
        compiler_params=_cparams(("parallel", "parallel")),
        name="inproj",
    )(x, sc.reshape(bsz, 1, d), sh.reshape(bsz, 1, d), g.reshape(1, d), w_rope, w_rot, w_plain,
      c128, s128, c64, s64)
    names = [n for (n, _, _, _) in _ROPE_SEGS] + [n for (n, _, _) in _PLAIN_SEGS]
    return dict(zip(names, outs))


def _ret_kernel(q_ref, k_ref, v_ref, g_ref, dmat_ref, xi_ref, zeta_ref, cd_ref, o_ref, state_ref):
    @pl.when(pl.program_id(1) == 0)
    def _():
        state_ref[...] = jnp.zeros_like(state_ref)

    for h in range(RET_HEADS):
        sl = slice(h * RET_DK, (h + 1) * RET_DK)
        q = q_ref[0, :, sl]
        k = k_ref[0, :, sl]
        v = v_ref[0, :, sl]
        state = state_ref[h]
        scores = _dot_nt(q, k) * dmat_ref[h]
        inner = _dot(scores.astype(BF16), v)
        cross = _dot(q, state.astype(BF16)) * xi_ref[h]
        y = inner + cross
        mu = jnp.mean(y, axis=-1, keepdims=True)
        yc = y - mu
        var = jnp.mean(yc * yc, axis=-1, keepdims=True)
        yn = yc * lax.rsqrt(var + NORM_EPS)
        g = g_ref[0, :, sl]
        o_ref[0, :, sl] = (g * jax.nn.sigmoid(g) * yn).astype(o_ref.dtype)
        kz = (k.astype(F32) * zeta_ref[h]).astype(BF16)
        state_ref[h] = cd_ref[h] * state + _dot_tn(kz, v)


def _retention(q, k, v, g):
    bsz, seq, _ = q.shape
    c = RET_CHUNK
    nh = RET_HEADS
    log_g = jnp.log1p(-jnp.exp2(-5.0 - jnp.arange(nh, dtype=F32)))
    n = jnp.arange(c, dtype=F32)
    diff = n[:, None] - n[None, :]
    causal = diff >= 0
    dmat = jnp.where(causal[None], jnp.exp(jnp.where(causal, diff, 0.0)[None] * log_g[:, None, None]), 0.0)
    zeta = jnp.exp((c - 1 - n)[None, :] * log_g[:, None])
    xi = jnp.exp((n + 1.0)[None, :] * log_g[:, None])
    cdecay = jnp.exp(c * log_g)
    xi_b = jnp.broadcast_to(xi[:, :, None], (nh, c, RET_DV))
    zeta_b = jnp.broadcast_to(zeta[:, :, None], (nh, c, RET_DK))
    cd_b = jnp.broadcast_to(cdecay[:, None, None], (nh, 1, RET_DV))
    width = nh * RET_DK
    blk = pl.BlockSpec((1, c, width), lambda b, i: (b, i, 0))
    const3 = lambda b, i: (0, 0, 0)
    return pl.pallas_call(
        _ret_kernel,
        grid=(bsz, seq // c),
        in_specs=[blk, blk, blk, blk,
                  pl.BlockSpec((nh, c, c), const3),
                  pl.BlockSpec((nh, c, RET_DV), const3),
                  pl.BlockSpec((nh, c, RET_DK), const3),
                  pl.BlockSpec((nh, 1, RET_DV), const3)],
        out_specs=blk,
        out_shape=jax.ShapeDtypeStruct((bsz, seq, width), BF16),
        scratch_shapes=[pltpu.VMEM((nh, RET_DK, RET_DV), F32)],
        compiler_params=_cparams(("parallel", "arbitrary")),
        name="ret",
    )(q, k, v, g, dmat, xi_b, zeta_b, cd_b)


def _cmp_kernel(ncb, xk_ref, xv_ref, wk1_ref, wv1_ref, pk_ref, pv_ref, bk_ref, bv_ref, wk2_ref, wv2_ref,
                ko_ref, vo_ref, shift_ref):
    nrow = xk_ref.shape[1]
    row = lax.broadcasted_iota(jnp.int32, (nrow, 1), 0)
    for (x_ref, w1_ref, p_ref, b_ref, w2_ref, o_ref) in (
            (xk_ref, wk1_ref, pk_ref, bk_ref, wk2_ref, ko_ref),
            (xv_ref, wv1_ref, pv_ref, bv_ref, wv2_ref, vo_ref)):
        a = _dot(x_ref[0], w1_ref[...])
        pw = _dot(p_ref[...], w1_ref[...])
        for g in range(NSA_KV_GROUPS):
            c0 = g * 2 * CMP_HID
            lo = a[:, c0:c0 + CMP_HID]
            hi = a[:, c0 + CMP_HID:c0 + 2 * CMP_HID]
            shift_ref[pl.ds(0, nrow), :] = hi
            shift_ref[pl.ds(nrow, 8), :] = jnp.zeros((8, CMP_HID), F32)
            hi_next = shift_ref[pl.ds(1, nrow), :]
            bias = pw[0:1, c0:c0 + CMP_HID] + pw[1:2, c0 + CMP_HID:c0 + 2 * CMP_HID] + b_ref[...]
            hid = jax.nn.gelu(lo + hi_next + bias)
            out = _dot(hid.astype(BF16), w2_ref[...])
            o_ref[0, g] = jnp.where(row < ncb, out, 0.0).astype(o_ref.dtype)


def _cmp_weights(w1, pe):
    hid = w1.shape[1]
    w1r = w1.reshape(2, CMP_STRIDE, NSA_DH, hid)
    blocks = []
    for g in range(NSA_KV_GROUPS):
        for half in range(2):
            blk = jnp.zeros((CMP_STRIDE, NSA_KV_GROUPS, NSA_DH, hid), F32).at[:, g].set(w1r[half])
            blocks.append(blk.reshape(CMP_STRIDE * NSA_KV_GROUPS * NSA_DH, hid))
    wbig = jnp.concatenate(blocks, axis=-1).astype(BF16)
    per = pe.reshape(2, CMP_STRIDE, 1, NSA_DH)
    pch = jnp.broadcast_to(per, (2, CMP_STRIDE, NSA_KV_GROUPS, NSA_DH)).reshape(2, -1)
    prow = jnp.zeros((16, pch.shape[1]), F32).at[:2].set(pch).astype(BF16)
    return wbig, prow


def _compress(kraw, vraw, pe_k, pe_v, wk1, bk1, wk2, wv1, bv1, wv2):
    bsz, seq, width = kraw.shape
    nrow = seq // CMP_STRIDE
    ncb = (seq - CMP_LEN) // CMP_STRIDE + 1
    xk = kraw.reshape(bsz, nrow, CMP_STRIDE * width)
    xv = vraw.reshape(bsz, nrow, CMP_STRIDE * width)
    wkb, pk = _cmp_weights(wk1, pe_k)
    wvb, pv = _cmp_weights(wv1, pe_v)
    xblk = pl.BlockSpec((1, nrow, CMP_STRIDE * width), lambda b: (b, 0, 0))
    c2 = lambda b: (0, 0)
    oblk = pl.BlockSpec((1, NSA_KV_GROUPS, nrow, NSA_DH), lambda b: (b, 0, 0, 0))
    osh = jax.ShapeDtypeStruct((bsz, NSA_KV_GROUPS, nrow, NSA_DH), BF16)
    return pl.pallas_call(
        functools.partial(_cmp_kernel, ncb),
        grid=(bsz,),
        in_specs=[xblk, xblk,
                  pl.BlockSpec(wkb.shape, c2), pl.BlockSpec(wvb.shape, c2),
                  pl.BlockSpec(pk.shape, c2), pl.BlockSpec(pv.shape, c2),
                  pl.BlockSpec((1, CMP_HID), c2), pl.BlockSpec((1, CMP_HID), c2),
                  pl.BlockSpec((CMP_HID, NSA_DH), c2), pl.BlockSpec((CMP_HID, NSA_DH), c2)],
        out_specs=[oblk, oblk],
        out_shape=[osh, osh],
        scratch_shapes=[pltpu.VMEM((nrow + 8, CMP_HID), F32)],
        compiler_params=_cparams(("parallel",)),
        name="cmp",
    )(xk, xv, wkb, wvb, pk, pv, bk1.reshape(1, -1), bv1.reshape(1, -1), wk2.astype(BF16), wv2.astype(BF16))


def _softmax_rows(s, mask):
    s = jnp.where(mask, s, -jnp.inf)
    m = jnp.max(s, axis=-1, keepdims=True)
    m = jnp.where(m > -jnp.inf, m, 0.0)
    e = jnp.where(mask, jnp.exp(s - m), 0.0)
    return e / jnp.maximum(jnp.sum(e, axis=-1, keepdims=True), jnp.finfo(F32).tiny)


def _cattn_kernel(q_ref, kc_ref, vc_ref, ov_ref, oc_ref, sel_ref):
    tq = q_ref.shape[1]
    nrow = kc_ref.shape[2]
    nsel = sel_ref.shape[3]
    n_top = min(SEL_TOPN, nsel)
    q0 = pl.program_id(1) * tq
    t = q0 + lax.broadcasted_iota(jnp.int32, (tq, 1), 0)
    cmp_end = lax.broadcasted_iota(jnp.int32, (1, nrow), 1) * CMP_STRIDE + (CMP_LEN - 1)
    cmask = cmp_end <= t
    blk = lax.broadcasted_iota(jnp.int32, (tq, nsel), 1)
    cur = t // SEL_BLOCK
    forced = (blk == 0) | (blk == cur) | (blk == cur - 1)
    for g in range(NSA_KV_GROUPS):
        kc = kc_ref[0, g]
        vc = vc_ref[0, g]
        imp = jnp.zeros((tq, nsel), F32)
        for h in range(NSA_HPG):
            hq = g * NSA_HPG + h
            sl = slice(hq * NSA_DH, (hq + 1) * NSA_DH)
            p = _softmax_rows(_dot_nt(q_ref[0, :, sl], kc), cmask)
            pb = p.astype(BF16)
            oc_ref[0, :, sl] = _dot(pb, vc)
            imp = imp + _dot(pb, ov_ref[...])
        v = jnp.where(forced, FORCED_SCORE, imp)
        v = jnp.where(blk <= cur, v, -jnp.inf)
        sel = jnp.zeros((tq, nsel), F32)
        for _ in range(n_top):
            m = jnp.max(v, axis=-1, keepdims=True)
            cand = (v == m) & (m > -jnp.inf)
            idx = jnp.min(jnp.where(cand, blk, nsel), axis=-1, keepdims=True)
            pick = blk == idx
            sel = jnp.where(pick, 1.0, sel)
            v = jnp.where(pick, -jnp.inf, v)
        sel_ref[0, g] = sel.astype(sel_ref.dtype)


def _cattn(nq, kcmp, vcmp, tq=128):
    bsz, seq, width = nq.shape
    nrow = kcmp.shape[2]
    nsel = seq // SEL_BLOCK
    cmp_start = jnp.arange(nrow) * CMP_STRIDE
    sel_start = jnp.arange(nsel) * SEL_BLOCK
    overlap = jnp.clip(jnp.minimum(cmp_start[:, None] + CMP_LEN, sel_start[None, :] + SEL_BLOCK)
                       - jnp.maximum(cmp_start[:, None], sel_start[None, :]), 0).astype(F32) / CMP_STRIDE
    kv = pl.BlockSpec((1, NSA_KV_GROUPS, nrow, NSA_DH), lambda b, i: (b, 0, 0, 0))
    return pl.pallas_call(
        _cattn_kernel,
        grid=(bsz, seq // tq),
        in_specs=[pl.BlockSpec((1, tq, width), lambda b, i: (b, i, 0)), kv, kv,
                  pl.BlockSpec((nrow, nsel), lambda b, i: (0, 0))],
        out_specs=[pl.BlockSpec((1, tq, width), lambda b, i: (b, i, 0)),
                   pl.BlockSpec((1, NSA_KV_GROUPS, tq, nsel), lambda b, i: (b, 0, i, 0))],
        out_shape=[jax.ShapeDtypeStruct((bsz, seq, width), F32),
                   jax.ShapeDtypeStruct((bsz, NSA_KV_GROUPS, seq, nsel), BF16)],
        compiler_params=_cparams(("parallel", "parallel")),
        name="cattn",
    )(nq, kcmp, vcmp, overlap.astype(BF16))


def _sattn_kernel(tk, q_ref, ks_ref, vs_ref, kw_ref, vw_ref, sel_ref, ex_ref, oc_ref, gate_ref, o_ref):
    tq = q_ref.shape[1]
    seq = ks_ref.shape[1]
    qi = pl.program_id(1)
    q0 = qi * tq
    t = q0 + lax.broadcasted_iota(jnp.int32, (tq, 1), 0)
    n_kt = (q0 + tq - 1) // tk + 1
    wlen = min(WINDOW + tq, seq)
    w0 = pl.multiple_of(jnp.maximum(q0 + tq - wlen, 0), 8)
    wpos = w0 + lax.broadcasted_iota(jnp.int32, (1, wlen), 1)
    wbias = jnp.where((wpos <= t) & (wpos > t - WINDOW), 0.0, MASK_NEG)
    gate = jax.nn.sigmoid(gate_ref[0])
    for g in range(NSA_KV_GROUPS):
        gs = slice(g * NSA_DH, (g + 1) * NSA_DH)
        qs = jnp.concatenate(
            [q_ref[0, :, (g * NSA_HPG + h) * NSA_DH:(g * NSA_HPG + h + 1) * NSA_DH] for h in range(NSA_HPG)],
            axis=0)
        sel = sel_ref[0, g]

        def body(kt, carry):
            m, l, acc = carry
            k0 = pl.multiple_of(kt * tk, tk)
            kblk = ks_ref[0, pl.ds(k0, tk), gs]
            vblk = vs_ref[0, pl.ds(k0, tk), gs]
            kpos = k0 + lax.broadcasted_iota(jnp.int32, (1, tk), 1)
            allowed = (_dot(sel, ex_ref[kt]) > 0.5) & (kpos <= t)
            bias = jnp.where(allowed, 0.0, MASK_NEG)
            s = _dot_nt(qs, kblk).reshape(NSA_HPG, tq, tk) + bias[None]
            m_new = jnp.maximum(m, jnp.max(s, axis=-1, keepdims=True))
            alpha = jnp.exp(m - m_new)
            p = jnp.exp(s - m_new)
            l_new = alpha * l + jnp.sum(p, axis=-1, keepdims=True)
            pv = _dot(p.reshape(NSA_HPG * tq, tk).astype(BF16), vblk).reshape(NSA_HPG, tq, NSA_DH)
            return m_new, l_new, alpha * acc + pv

        init = (jnp.full((NSA_HPG, tq, 1), MASK_NEG, F32),
                jnp.zeros((NSA_HPG, tq, 1), F32),
                jnp.zeros((NSA_HPG, tq, NSA_DH), F32))
        m, l, acc = lax.fori_loop(0, n_kt, body, init)
        o_s = acc / l

        kwb = kw_ref[0, pl.ds(w0, wlen), gs]
        vwb = vw_ref[0, pl.ds(w0, wlen), gs]
        sw = _dot_nt(qs, kwb).reshape(NSA_HPG, tq, wlen) + wbias[None]
        mw = jnp.max(sw, axis=-1, keepdims=True)
        pw = jnp.exp(sw - mw)
        lw = jnp.sum(pw, axis=-1, keepdims=True)
        o_w = _dot(pw.reshape(NSA_HPG * tq, wlen).astype(BF16), vwb).reshape(NSA_HPG, tq, NSA_DH) / lw

        for h in range(NSA_HPG):
            hq = g * NSA_HPG + h
            sl = slice(hq * NSA_DH, (hq + 1) * NSA_DH)
            out = (gate[:, 3 * hq:3 * hq + 1] * oc_ref[0, :, sl]
                   + gate[:, 3 * hq + 1:3 * hq + 2] * o_s[h]
                   + gate[:, 3 * hq + 2:3 * hq + 3] * o_w[h])
            o_ref[0, :, sl] = out.astype(o_ref.dtype)


def _sattn(nq, ks, vs, kw, vw, sel, oc, gate, tq=128, tk=512):
    bsz, seq, width = nq.shape
    nsel = sel.shape[3]
    tk = min(tk, seq)
    expand = (jnp.arange(seq)[None, :] // SEL_BLOCK == jnp.arange(nsel)[:, None]).astype(BF16)
    expand = expand.reshape(nsel, seq // tk, tk).transpose(1, 0, 2)
    qblk = pl.BlockSpec((1, tq, width), lambda b, i: (b, i, 0))
    kvblk = pl.BlockSpec((1, seq, LANES), lambda b, i: (b, 0, 0))
    return pl.pallas_call(
        functools.partial(_sattn_kernel, tk),
        grid=(bsz, seq // tq),
        in_specs=[qblk, kvblk, kvblk, kvblk, kvblk,
                  pl.BlockSpec((1, NSA_KV_GROUPS, tq, nsel), lambda b, i: (b, 0, i, 0)),
                  pl.BlockSpec((seq // tk, nsel, tk), lambda b, i: (0, 0, 0)),
                  qblk,
                  pl.BlockSpec((1, tq, LANES), lambda b, i: (b, i, 0))],
        out_specs=qblk,
        out_shape=jax.ShapeDtypeStruct((bsz, seq, width), BF16),
        compiler_params=_cparams(("parallel", "parallel")),
        name="sattn",
    )(nq, ks, vs, kw, vw, sel, expand, oc, gate)


_NEXT = PEER_TOPK + 1
_VROWS = 24


def _extract_desc(v, count, rows_out):
    nrow, tm = v.shape
    rid = lax.broadcasted_iota(jnp.int32, (nrow, tm), 0)
    oid = lax.broadcasted_iota(jnp.int32, (rows_out, tm), 0)
    out = jnp.full((rows_out, tm), -jnp.inf, F32)
    for k in range(count):
        m = jnp.max(v, axis=0, keepdims=True)
        idx = jnp.min(jnp.where(v == m, rid, nrow), axis=0, keepdims=True)
        v = jnp.where(rid == idx, -jnp.inf, v)
        out = jnp.where(oid == k, m, out)
    return out


def _pq_kernel(ret_ref, nsa_ref, x_ref, ga_ref, sc_ref, sh_ref, g_ref, wo1_ref, wo2_ref, wq_ref, sk_ref,
               x1_ref, h2_ref, s1_ref, e1_ref, th_ref, e0_ref):
    mix = _dot(ret_ref[0], wo1_ref[...]) + _dot(nsa_ref[0], wo2_ref[...])
    x1 = x_ref[0] + ga_ref[0] * mix
    x1_ref[0] = x1
    h2 = _norm_mod(x1, g_ref[...], sc_ref[0], sh_ref[0]).astype(BF16)
    h2_ref[0] = h2
    qp = _dot(h2, wq_ref[...]).astype(BF16)
    dh = sk_ref.shape[3]
    tm = qp.shape[0]
    rid8 = lax.broadcasted_iota(jnp.int32, (8, tm), 0)
    for p in range(PEER_HEADS):
        s0 = _dot_nt(sk_ref[p, 0], qp[:, (2 * p) * dh:(2 * p + 1) * dh])
        s1 = _dot_nt(sk_ref[p, 1], qp[:, (2 * p + 1) * dh:(2 * p + 2) * dh])
        v0 = _extract_desc(s0, _NEXT, _VROWS)
        v1 = _extract_desc(s1, _NEXT, _VROWS)
        slabs = [v0[0:1] + v1]
        for a in range(1, 8):
            nb = _NEXT // (a + 1)
            slabs.append(jnp.where(rid8 < nb, v0[a:a + 1] + v1[0:8], -jnp.inf))
        slabs.append(v0[8:_VROWS] + v1[0:1])
        cand = jnp.concatenate(slabs, axis=0)
        top = _extract_desc(cand, _NEXT, _VROWS)
        mx = top[0:1]
        kid = lax.broadcasted_iota(jnp.int32, top.shape, 0)
        z = jnp.sum(jnp.where(kid < PEER_TOPK, jnp.exp(top - mx), 0.0), axis=0, keepdims=True)
        tau = 0.5 * (top[PEER_TOPK - 1:PEER_TOPK] + top[PEER_TOPK:PEER_TOPK + 1])
        s1_ref[p] = s1
        e1_ref[p] = jnp.exp(s1 - v1[0:1])
        th_ref[p] = tau - s0
        e0_ref[p] = jnp.exp(s0 - v0[0:1]) / z


def _pq(ret, nsa, x, ga1, sc2, sh2, g, w_out, w_q, subkeys, tm=256):
    bsz, seq, d = x.shape
    tm = min(tm, seq)
    ntok = bsz * seq
    nkeys = subkeys.shape[2]
    nq = w_q.shape[1]
    wr = ret.shape[2]
    wo1 = w_out[:wr].astype(BF16)
    wo2 = w_out[wr:].astype(BF16)
    nt = seq // tm
    tokblk = lambda w: pl.BlockSpec((1, tm, w), lambda b, i: (b, i, 0))
    modblk = pl.BlockSpec((1, 1, d), lambda b, i: (b, 0, 0))
    c2 = lambda b, i: (0, 0)
    tblk = pl.BlockSpec((PEER_HEADS, nkeys, tm), lambda b, i: (0, 0, b * nt + i))
    tsh = jax.ShapeDtypeStruct((PEER_HEADS, nkeys, ntok), F32)
    return pl.pallas_call(
        _pq_kernel,
        grid=(bsz, nt),
        in_specs=[tokblk(wr), tokblk(nsa.shape[2]), tokblk(d), modblk, modblk, modblk,
                  pl.BlockSpec((1, d), c2),
                  pl.BlockSpec(wo1.shape, c2), pl.BlockSpec(wo2.shape, c2),
                  pl.BlockSpec((d, nq), c2, pipeline_mode=pl.Buffered(1)),
                  pl.BlockSpec(subkeys.shape, lambda b, i: (0, 0, 0, 0))],
        out_specs=[tokblk(d), tokblk(d), tblk, tblk, tblk, tblk],
        out_shape=[jax.ShapeDtypeStruct((bsz, seq, d), F32),
                   jax.ShapeDtypeStruct((bsz, seq, d), BF16),
                   tsh, tsh, tsh, tsh],
        compiler_params=_cparams(("parallel", "parallel")),
        name="pq",
    )(ret, nsa, x, ga1.reshape(bsz, 1, d), sc2.reshape(bsz, 1, d), sh2.reshape(bsz, 1, d), g.reshape(1, d),
      wo1, wo2, w_q.astype(BF16), subkeys.astype(BF16))


def _peer_kernel(h_ref, dn_ref, up_ref, s1_ref, e1_ref, th_ref, e0_ref, o_ref, c_ref):
    j = pl.program_id(1)

    @pl.when(j == 0)
    def _():
        o_ref[...] = jnp.zeros_like(o_ref)

    nkeys = s1_ref.shape[1]
    te = dn_ref.shape[0]
    a_t = _dot_nt(dn_ref[...], h_ref[...])
    for ii in range(te // nkeys):
        rows = slice(ii * nkeys, (ii + 1) * nkeys)
        w = None
        for p in range(PEER_HEADS):
            contrib = jnp.where(s1_ref[p] >= th_ref[p, ii:ii + 1, :], e1_ref[p], 0.0) * e0_ref[p, ii:ii + 1, :]
            w = contrib if w is None else w + contrib
        c_ref[rows, :] = (w * jax.nn.gelu(a_t[rows])).astype(BF16)
    o_ref[...] += _dot(up_ref[...], c_ref[...])


def _peer(h2, down, up_t, s1, e1, th, e0, tm=512, te=1024):
    ntok, d = h2.shape
    nexp = down.shape[0]
    nkeys = s1.shape[1]
    tm = min(tm, ntok)
    tblk = pl.BlockSpec((PEER_HEADS, nkeys, tm), lambda i, j: (0, 0, i))
    iblk = pl.BlockSpec((PEER_HEADS, te // nkeys, tm), lambda i, j: (0, j, i))
    return pl.pallas_call(
        _peer_kernel,
        grid=(ntok // tm, nexp // te),
        in_specs=[pl.BlockSpec((tm, d), lambda i, j: (i, 0)),
                  pl.BlockSpec((te, d), lambda i, j: (j, 0)),
                  pl.BlockSpec((d, te), lambda i, j: (0, j)),
                  tblk, tblk, iblk, iblk],
        out_specs=pl.BlockSpec((d, tm), lambda i, j: (0, i)),
        out_shape=jax.ShapeDtypeStruct((d, ntok), F32),
        scratch_shapes=[pltpu.VMEM((te, tm), BF16)],
        compiler_params=_cparams(("parallel", "arbitrary")),
        name="peer",
    )(h2, down, up_t, s1, e1, th, e0)


def _final_kernel(normalize, x_ref, p_ref, ga_ref, g_ref, o_ref):
    x = x_ref[0] + ga_ref[0] * p_ref[...].T
    if normalize:
        x = x * lax.rsqrt(jnp.mean(x * x, axis=-1, keepdims=True) + NORM_EPS) * g_ref[...]
    o_ref[0] = x


def _final(x1, peer_t, ga2, g, normalize, tm=256):
    bsz, seq, d = x1.shape
    tm = min(tm, seq)
    nt = seq // tm
    return pl.pallas_call(
        functools.partial(_final_kernel, normalize),
        grid=(bsz, nt),
        in_specs=[pl.BlockSpec((1, tm, d), lambda b, i: (b, i, 0)),
                  pl.BlockSpec((d, tm), lambda b, i: (0, b * nt + i)),
                  pl.BlockSpec((1, 1, d), lambda b, i: (b, 0, 0)),
                  pl.BlockSpec((1, d), lambda b, i: (0, 0))],
        out_specs=pl.BlockSpec((1, tm, d), lambda b, i: (b, i, 0)),
        out_shape=jax.ShapeDtypeStruct((bsz, seq, d), F32),
        compiler_params=_cparams(("parallel", "parallel")),
        name="final",
    )(x1, peer_t, ga2.reshape(bsz, 1, d), g.reshape(1, d))


def _layer(x, mod, g_mix, w_in, pe_k, pe_v, wk1, bk1, wk2, wv1, bv1, wv2, w_out, g_ffn, w_q, subkeys, down, up):
    bsz, seq, d = x.shape
    sh1, sc1, ga1, sh2, sc2, ga2 = jnp.split(mod, 6, axis=-1)
    pr = _inproj(x, sc1, sh1, g_mix, w_in)
    ret = _retention(pr["rq"], pr["rk"], pr["rv"], pr["rg"])
    kcmp, vcmp = _compress(pr["kc"], pr["vc"], pe_k, pe_v, wk1, bk1, wk2, wv1, bv1, wv2)
    oc, sel = _cattn(pr["nq"], kcmp, vcmp)
    nsa = _sattn(pr["nq"], pr["ks"], pr["vs"], pr["kw"], pr["vw"], sel, oc, pr["gate"])
    x1, h2, s1, e1, th, e0 = _pq(ret, nsa, x, ga1, sc2, sh2, g_ffn, w_out, w_q, subkeys)
    peer_t = _peer(h2.reshape(bsz * seq, d), down.astype(BF16), up.T.astype(BF16), s1, e1, th, e0)
    return x1, peer_t, ga2


def kernel(x, c, w_ada, b_ada, g_norm_mix, w_in, pe_cmp_k, pe_cmp_v, w_cmp_k1, b_cmp_k1, w_cmp_k2, w_cmp_v1,
           b_cmp_v1, w_cmp_v2, w_out, g_norm_ffn, w_peer_q, peer_subkeys, peer_down, peer_up, g_norm_final):
    depth = w_ada.shape[0]
    for l in range(depth):
        mod = _ada(c, w_ada[l], b_ada[l])
        x1, peer_t, ga2 = _layer(x, mod, g_norm_mix[l], w_in[l], pe_cmp_k[l], pe_cmp_v[l], w_cmp_k1[l],
                                 b_cmp_k1[l], w_cmp_k2[l], w_cmp_v1[l], b_cmp_v1[l], w_cmp_v2[l], w_out[l],
                                 g_norm_ffn[l], w_peer_q[l], peer_subkeys[l], peer_down[l], peer_up[l])
        x = _final(x1, peer_t, ga2, g_norm_final, normalize=(l == depth - 1))
    return x
```

```python
import functools
import math

import jax
import jax.numpy as jnp
from jax import lax
from jax.experimental import pallas as pl
from jax.experimental.pallas import tpu as pltpu

F32 = jnp.float32
BF16 = jnp.bfloat16

RET_HEADS = 4
RET_DK = 128
RET_DV = 128
RET_CHUNK = 128
NSA_HEADS = 8
NSA_KV_GROUPS = 2
NSA_HPG = NSA_HEADS // NSA_KV_GROUPS
NSA_DH = 64
CMP_LEN = 32
CMP_STRIDE = 16
CMP_HID = 256
SEL_BLOCK = 64
SEL_TOPN = 16
WINDOW = 512
FORCED_SCORE = 1e9
ROPE_THETA = 10000.0
PEER_HEADS = 8
PEER_TOPK = 16
NORM_EPS = 1e-6

LANES = 128
MASK_NEG = -1e30
VMEM_LIMIT = 56 * 1024 * 1024


def _cparams(sem):
    return pltpu.CompilerParams(dimension_semantics=sem, vmem_limit_bytes=VMEM_LIMIT)


def _dot(a, b):
    return jnp.dot(a, b, preferred_element_type=F32)


def _dot_nt(a, b):
    return lax.dot_general(a, b, (((1,), (1,)), ((), ())), preferred_element_type=F32)


def _dot_tn(a, b):
    return lax.dot_general(a, b, (((0,), (0,)), ((), ())), preferred_element_type=F32)


def _split_bf16(a):
    hi = a.astype(BF16)
    lo = (a - hi.astype(F32)).astype(BF16)
    return hi, lo


def _ada_kernel(c_ref, w_ref, b_ref, o_ref):
    c = c_ref[...]
    a = c * jax.nn.sigmoid(c)
    a_hi, a_lo = _split_bf16(a)
    w_hi, w_lo = _split_bf16(w_ref[...])
    o_ref[...] = _dot(a_hi, w_hi) + _dot(a_hi, w_lo) + _dot(a_lo, w_hi) + b_ref[...]


def _ada(c, w, b):
    bsz, d = c.shape
    n = w.shape[1]
    rows = 16
    cp = jnp.zeros((rows, d), F32).at[:bsz].set(c)
    tn = 1024
    out = pl.pallas_call(
        _ada_kernel,
        grid=(n // tn,),
        in_specs=[pl.BlockSpec((rows, d), lambda j: (0, 0)),
                  pl.BlockSpec((d, tn), lambda j: (0, j)),
                  pl.BlockSpec((1, tn), lambda j: (0, j))],
        out_specs=pl.BlockSpec((rows, tn), lambda j: (0, j)),
        out_shape=jax.ShapeDtypeStruct((rows, n), F32),
        compiler_params=_cparams(("parallel",)),
        name="ada",
    )(cp, w, b.reshape(1, n))
    return out[:bsz]


_ROPE_SEGS = (("rq", RET_HEADS * RET_DK, RET_DK, 1.0),
              ("rk", RET_HEADS * RET_DK, RET_DK, RET_DK ** -0.5),
              ("nq", NSA_HEADS * NSA_DH, NSA_DH, NSA_DH ** -0.5),
              ("kc", LANES, NSA_DH, 1.0),
              ("ks", LANES, NSA_DH, 1.0),
              ("kw", LANES, NSA_DH, 1.0))
_PLAIN_SEGS = (("rv", RET_HEADS * RET_DV, BF16),
               ("rg", RET_HEADS * RET_DV, F32),
               ("vc", LANES, BF16),
               ("vs", LANES, BF16),
               ("vw", LANES, BF16),
               ("gate", LANES, F32))


def _norm_mod(x, g, sc, sh):
    y = x * lax.rsqrt(jnp.mean(x * x, axis=-1, keepdims=True) + NORM_EPS)
    return (y * g) * (1.0 + sc) + sh


def _inproj_kernel(x_ref, sc_ref, sh_ref, g_ref, w_ref, wrot_ref, wpl_ref,
                   c128_ref, s128_ref, c64_ref, s64_ref, *out_refs):
    h = _norm_mod(x_ref[0], g_ref[...], sc_ref[0], sh_ref[0])
    hb = h.astype(BF16)
    outs = list(out_refs)
    off = 0
    for (_, width, hd, scale) in _ROPE_SEGS:
        o_ref = outs.pop(0)
        p = _dot(hb, w_ref[:, off:off + width])
        pr = _dot(hb, wrot_ref[:, off:off + width])
        cos = c128_ref[...] if hd == LANES else c64_ref[...]
        sin = s128_ref[...] if hd == LANES else s64_ref[...]
        for cb in range(width // LANES):
            sl = slice(cb * LANES, (cb + 1) * LANES)
            o = p[:, sl] * cos + pr[:, sl] * sin
            if scale != 1.0:
                o = o * scale
            o_ref[0, :, sl] = o.astype(o_ref.dtype)
        off += width
    off = 0
    for (_, width, _) in _PLAIN_SEGS:
        o_ref = outs.pop(0)
        o_ref[0] = _dot(hb, wpl_ref[:, off:off + width]).astype(o_ref.dtype)
        off += width


def _rot_cols(wseg, hd):
    d = wseg.shape[0]
    w3 = wseg.reshape(d, -1, hd)
    half = hd // 2
    return jnp.concatenate([-w3[..., half:], w3[..., :half]], axis=-1).reshape(d, -1)


def _rope_tables(seq, hd):
    pos = jnp.arange(seq)
    inv = ROPE_THETA ** (-jnp.arange(0, hd, 2, dtype=F32) / hd)
    ang = pos.astype(F32)[:, None] * inv[None, :]
    reps = 2 * LANES // hd
    return jnp.tile(jnp.cos(ang), (1, reps)), jnp.tile(jnp.sin(ang), (1, reps))


def _inproj(x, sc, sh, g, w_in, tm=512):
    bsz, seq, d = x.shape
    splits = (RET_HEADS * RET_DK, RET_HEADS * RET_DK, RET_HEADS * RET_DV, RET_HEADS * RET_DV,
              NSA_HEADS * NSA_DH) + (NSA_KV_GROUPS * NSA_DH,) * 6 + (NSA_HEADS * 3,)
    pts = []
    acc = 0
    for s in splits[:-1]:
        acc += s
        pts.append(acc)
    (r_q, r_k, r_v, r_g, n_q, n_kc, n_vc, n_ks, n_vs, n_kw, n_vw, n_gate) = jnp.split(w_in, pts, axis=-1)
    rope_w = {"rq": r_q, "rk": r_k, "nq": n_q, "kc": n_kc, "ks": n_ks, "kw": n_kw}
    w_rope = jnp.concatenate([rope_w[n] for (n, _, _, _) in _ROPE_SEGS], axis=-1).astype(BF16)
    w_rot = jnp.concatenate([_rot_cols(rope_w[n], hd) for (n, _, hd, _) in _ROPE_SEGS], axis=-1).astype(BF16)
    gate_pad = jnp.zeros((d, LANES), F32).at[:, :NSA_HEADS * 3].set(n_gate)
    w_plain = jnp.concatenate([r_v, r_g, n_vc, n_vs, n_vw, gate_pad], axis=-1).astype(BF16)
    c128, s128 = _rope_tables(seq, RET_DK)
    c64, s64 = _rope_tables(seq, NSA_DH)

    tm = min(tm, seq)
    nrope = w_rope.shape[1]
    npl = w_plain.shape[1]
    const2 = lambda b, i: (0, 0)
    tab = pl.BlockSpec((tm, LANES), lambda b, i: (i, 0))
    in_specs = [pl.BlockSpec((1, tm, d), lambda b, i: (b, i, 0)),
                pl.BlockSpec((1, 1, d), lambda b, i: (b, 0, 0)),
                pl.BlockSpec((1, 1, d), lambda b, i: (b, 0, 0)),
                pl.BlockSpec((1, d), const2),
                pl.BlockSpec((d, nrope), const2, pipeline_mode=pl.Buffered(1)),
                pl.BlockSpec((d, nrope), const2, pipeline_mode=pl.Buffered(1)),
                pl.BlockSpec((d, npl), const2, pipeline_mode=pl.Buffered(1)),
                tab, tab, tab, tab]
    out_specs = []
    out_shape = []
    for (_, width, _, _) in _ROPE_SEGS:
        out_specs.append(pl.BlockSpec((1, tm, width), lambda b, i: (b, i, 0)))
        out_shape.append(jax.ShapeDtypeStruct((bsz, seq, width), BF16))
    for (_, width, dt) in _PLAIN_SEGS:
        out_specs.append(pl.BlockSpec((1, tm, width), lambda b, i: (b, i, 0)))
        out_shape.append(jax.ShapeDtypeStruct((bsz, seq, width), dt))
    outs = pl.pallas_call(
        _inproj_kernel,
        grid=(bsz, seq // tm),
        in_specs=in_specs,
        out_specs=out_specs,
        out_shape=out_shape,
        compiler_params=_cparams(("parallel", "parallel")),
        name="inproj",
    )(x, sc.reshape(bsz, 1, d), sh.reshape(bsz, 1, d), g.reshape(1, d), w_rope, w_rot, w_plain,
      c128, s128, c64, s64)
    names = [n for (n, _, _, _) in _ROPE_SEGS] + [n for (n, _, _) in _PLAIN_SEGS]
    return dict(zip(names, outs))


def _ret_kernel(q_ref, k_ref, v_ref, g_ref, dmat_ref, xi_ref, zeta_ref, cd_ref, o_ref, state_ref):
    @pl.when(pl.program_id(1) == 0)
    def _():
        state_ref[...] = jnp.zeros_like(state_ref)

    for h in range(RET_HEADS):
        sl = slice(h * RET_DK, (h + 1) * RET_DK)
        q = q_ref[0, :, sl]
        k = k_ref[0, :, sl]
        v = v_ref[0, :, sl]
        state = state_ref[h]
        scores = _dot_nt(q, k) * dmat_ref[h]
        inner = _dot(scores.astype(BF16), v)
        cross = _dot(q, state.astype(BF16)) * xi_ref[h]
        y = inner + cross
        mu = jnp.mean(y, axis=-1, keepdims=True)
        yc = y - mu
        var = jnp.mean(yc * yc, axis=-1, keepdims=True)
        yn = yc * lax.rsqrt(var + NORM_EPS)
        g = g_ref[0, :, sl]
        o_ref[0, :, sl] = (g * jax.nn.sigmoid(g) * yn).astype(o_ref.dtype)
        kz = (k.astype(F32) * zeta_ref[h]).astype(BF16)
        state_ref[h] = cd_ref[h] * state + _dot_tn(kz, v)


def _retention(q, k, v, g):
    bsz, seq, _ = q.shape
    c = RET_CHUNK
    nh = RET_HEADS
    log_g = jnp.log1p(-jnp.exp2(-5.0 - jnp.arange(nh, dtype=F32)))
    n = jnp.arange(c, dtype=F32)
    diff = n[:, None] - n[None, :]
    causal = diff >= 0
    dmat = jnp.where(causal[None], jnp.exp(jnp.where(causal, diff, 0.0)[None] * log_g[:, None, None]), 0.0)
    zeta = jnp.exp((c - 1 - n)[None, :] * log_g[:, None])
    xi = jnp.exp((n + 1.0)[None, :] * log_g[:, None])
    cdecay = jnp.exp(c * log_g)
    xi_b = jnp.broadcast_to(xi[:, :, None], (nh, c, RET_DV))
    zeta_b = jnp.broadcast_to(zeta[:, :, None], (nh, c, RET_DK))
    cd_b = jnp.broadcast_to(cdecay[:, None, None], (nh, 1, RET_DV))
    width = nh * RET_DK
    blk = pl.BlockSpec((1, c, width), lambda b, i: (b, i, 0))
    const3 = lambda b, i: (0, 0, 0)
    return pl.pallas_call(
        _ret_kernel,
        grid=(bsz, seq // c),
        in_specs=[blk, blk, blk, blk,
                  pl.BlockSpec((nh, c, c), const3),
                  pl.BlockSpec((nh, c, RET_DV), const3),
                  pl.BlockSpec((nh, c, RET_DK), const3),
                  pl.BlockSpec((nh, 1, RET_DV), const3)],
        out_specs=blk,
        out_shape=jax.ShapeDtypeStruct((bsz, seq, width), BF16),
        scratch_shapes=[pltpu.VMEM((nh, RET_DK, RET_DV), F32)],
        compiler_params=_cparams(("parallel", "arbitrary")),
        name="ret",
    )(q, k, v, g, dmat, xi_b, zeta_b, cd_b)


def _cmp_kernel(ncb, xk_ref, xv_ref, wk1_ref, wv1_ref, pk_ref, pv_ref, bk_ref, bv_ref, wk2_ref, wv2_ref,
                ko_ref, vo_ref, shift_ref):
    nrow = xk_ref.shape[1]
    row = lax.broadcasted_iota(jnp.int32, (nrow, 1), 0)
    for (x_ref, w1_ref, p_ref, b_ref, w2_ref, o_ref) in (
            (xk_ref, wk1_ref, pk_ref, bk_ref, wk2_ref, ko_ref),
            (xv_ref, wv1_ref, pv_ref, bv_ref, wv2_ref, vo_ref)):
        a = _dot(x_ref[0], w1_ref[...])
        pw = _dot(p_ref[...], w1_ref[...])
        for g in range(NSA_KV_GROUPS):
            c0 = g * 2 * CMP_HID
            lo = a[:, c0:c0 + CMP_HID]
            hi = a[:, c0 + CMP_HID:c0 + 2 * CMP_HID]
            shift_ref[pl.ds(0, nrow), :] = hi
            shift_ref[pl.ds(nrow, 8), :] = jnp.zeros((8, CMP_HID), F32)
            hi_next = shift_ref[pl.ds(1, nrow), :]
            bias = pw[0:1, c0:c0 + CMP_HID] + pw[1:2, c0 + CMP_HID:c0 + 2 * CMP_HID] + b_ref[...]
            hid = jax.nn.gelu(lo + hi_next + bias)
            out = _dot(hid.astype(BF16), w2_ref[...])
            o_ref[0, g] = jnp.where(row < ncb, out, 0.0).astype(o_ref.dtype)


def _cmp_weights(w1, pe):
    hid = w1.shape[1]
    w1r = w1.reshape(2, CMP_STRIDE, NSA_DH, hid)
    blocks = []
    for g in range(NSA_KV_GROUPS):
        for half in range(2):
            blk = jnp.zeros((CMP_STRIDE, NSA_KV_GROUPS, NSA_DH, hid), F32).at[:, g].set(w1r[half])
            blocks.append(blk.reshape(CMP_STRIDE * NSA_KV_GROUPS * NSA_DH, hid))
    wbig = jnp.concatenate(blocks, axis=-1).astype(BF16)
    per = pe.reshape(2, CMP_STRIDE, 1, NSA_DH)
    pch = jnp.broadcast_to(per, (2, CMP_STRIDE, NSA_KV_GROUPS, NSA_DH)).reshape(2, -1)
    prow = jnp.zeros((16, pch.shape[1]), F32).at[:2].set(pch).astype(BF16)
    return wbig, prow


def _compress(kraw, vraw, pe_k, pe_v, wk1, bk1, wk2, wv1, bv1, wv2):
    bsz, seq, width = kraw.shape
    nrow = seq // CMP_STRIDE
    ncb = (seq - CMP_LEN) // CMP_STRIDE + 1
    xk = kraw.reshape(bsz, nrow, CMP_STRIDE * width)
    xv = vraw.reshape(bsz, nrow, CMP_STRIDE * width)
    wkb, pk = _cmp_weights(wk1, pe_k)
    wvb, pv = _cmp_weights(wv1, pe_v)
    xblk = pl.BlockSpec((1, nrow, CMP_STRIDE * width), lambda b: (b, 0, 0))
    c2 = lambda b: (0, 0)
    oblk = pl.BlockSpec((1, NSA_KV_GROUPS, nrow, NSA_DH), lambda b: (b, 0, 0, 0))
    osh = jax.ShapeDtypeStruct((bsz, NSA_KV_GROUPS, nrow, NSA_DH), BF16)
    return pl.pallas_call(
        functools.partial(_cmp_kernel, ncb),
        grid=(bsz,),
        in_specs=[xblk, xblk,
                  pl.BlockSpec(wkb.shape, c2), pl.BlockSpec(wvb.shape, c2),
                  pl.BlockSpec(pk.shape, c2), pl.BlockSpec(pv.shape, c2),
                  pl.BlockSpec((1, CMP_HID), c2), pl.BlockSpec((1, CMP_HID), c2),
                  pl.BlockSpec((CMP_HID, NSA_DH), c2), pl.BlockSpec((CMP_HID, NSA_DH), c2)],
        out_specs=[oblk, oblk],
        out_shape=[osh, osh],
        scratch_shapes=[pltpu.VMEM((nrow + 8, CMP_HID), F32)],
        compiler_params=_cparams(("parallel",)),
        name="cmp",
    )(xk, xv, wkb, wvb, pk, pv, bk1.reshape(1, -1), bv1.reshape(1, -1), wk2.astype(BF16), wv2.astype(BF16))


def _softmax_cols(s, mask):
    s = jnp.where(mask, s, -jnp.inf)
    m = jnp.max(s, axis=0, keepdims=True)
    m = jnp.where(m > -jnp.inf, m, 0.0)
    e = jnp.where(mask, jnp.exp(s - m), 0.0)
    return e / jnp.maximum(jnp.sum(e, axis=0, keepdims=True), jnp.finfo(F32).tiny)


def _cattn_kernel(q_ref, kc_ref, vc_ref, ovt_ref, oct_ref, selt_ref):
    tq = q_ref.shape[1]
    nrow = kc_ref.shape[2]
    nsel = selt_ref.shape[2]
    n_top = min(SEL_TOPN, nsel)
    q0 = pl.program_id(1) * tq
    t = q0 + lax.broadcasted_iota(jnp.int32, (1, tq), 1)
    cmp_end = lax.broadcasted_iota(jnp.int32, (nrow, 1), 0) * CMP_STRIDE + (CMP_LEN - 1)
    cmask = cmp_end <= t
    blk = lax.broadcasted_iota(jnp.int32, (nsel, tq), 0)
    cur = t // SEL_BLOCK
    forced = (blk == 0) | (blk == cur) | (blk == cur - 1)
    for g in range(NSA_KV_GROUPS):
        kc = kc_ref[0, g]
        vc = vc_ref[0, g]
        imp = jnp.zeros((nsel, tq), F32)
        for h in range(NSA_HPG):
            hq = g * NSA_HPG + h
            sl = slice(hq * NSA_DH, (hq + 1) * NSA_DH)
            p = _softmax_cols(_dot_nt(kc, q_ref[0, :, sl]), cmask)
            pb = p.astype(BF16)
            oct_ref[0, sl, :] = _dot_tn(vc, pb)
            imp = imp + _dot(ovt_ref[...], pb)
        v = jnp.where(forced, FORCED_SCORE, imp)
        v = jnp.where(blk <= cur, v, -jnp.inf)
        sel = jnp.zeros((nsel, tq), F32)
        for _ in range(n_top):
            m = jnp.max(v, axis=0, keepdims=True)
            cand = (v == m) & (m > -jnp.inf)
            idx = jnp.min(jnp.where(cand, blk, nsel), axis=0, keepdims=True)
            pick = blk == idx
            sel = jnp.where(pick, 1.0, sel)
            v = jnp.where(pick, -jnp.inf, v)
        selt_ref[0, g] = sel.astype(selt_ref.dtype)


def _cattn(nq, kcmp, vcmp, tq=256):
    bsz, seq, width = nq.shape
    nrow = kcmp.shape[2]
    nsel = seq // SEL_BLOCK
    tq = min(tq, seq)
    cmp_start = jnp.arange(nrow) * CMP_STRIDE
    sel_start = jnp.arange(nsel) * SEL_BLOCK
    overlap = jnp.clip(jnp.minimum(cmp_start[:, None] + CMP_LEN, sel_start[None, :] + SEL_BLOCK)
                       - jnp.maximum(cmp_start[:, None], sel_start[None, :]), 0).astype(F32) / CMP_STRIDE
    kv = pl.BlockSpec((1, NSA_KV_GROUPS, nrow, NSA_DH), lambda b, i: (b, 0, 0, 0))
    return pl.pallas_call(
        _cattn_kernel,
        grid=(bsz, seq // tq),
        in_specs=[pl.BlockSpec((1, tq, width), lambda b, i: (b, i, 0)), kv, kv,
                  pl.BlockSpec((nsel, nrow), lambda b, i: (0, 0))],
        out_specs=[pl.BlockSpec((1, width, tq), lambda b, i: (b, 0, i)),
                   pl.BlockSpec((1, NSA_KV_GROUPS, nsel, tq), lambda b, i: (b, 0, 0, i))],
        out_shape=[jax.ShapeDtypeStruct((bsz, width, seq), F32),
                   jax.ShapeDtypeStruct((bsz, NSA_KV_GROUPS, nsel, seq), BF16)],
        compiler_params=_cparams(("parallel", "parallel")),
        name="cattn",
    )(nq, kcmp, vcmp, overlap.T.astype(BF16))


def _sattn_kernel(tk, q_ref, ks_ref, vs_ref, kw_ref, vw_ref, selt_ref, ext_ref, oct_ref, gate_ref, o_ref):
    tq = q_ref.shape[1]
    seq = ks_ref.shape[1]
    q0 = pl.program_id(1) * tq
    t = q0 + lax.broadcasted_iota(jnp.int32, (1, tq), 1)
    n_kt = (q0 + tq - 1) // tk + 1
    wlen = min(WINDOW + tq, seq)
    w0 = pl.multiple_of(jnp.maximum(q0 + tq - wlen, 0), 8)
    wpos = w0 + lax.broadcasted_iota(jnp.int32, (wlen, 1), 0)
    wbias = jnp.where((wpos <= t) & (wpos > t - WINDOW), 0.0, MASK_NEG)
    wbias = jnp.concatenate([wbias] * NSA_HPG, axis=1)
    gate_t = jax.nn.sigmoid(gate_ref[0]).T
    qs_g = [jnp.concatenate(
        [q_ref[0, :, (g * NSA_HPG + h) * NSA_DH:(g * NSA_HPG + h + 1) * NSA_DH] for h in range(NSA_HPG)],
        axis=0) for g in range(NSA_KV_GROUPS)]

    def body(kt, carry):
        k0 = pl.multiple_of(kt * tk, tk)
        kpos = k0 + lax.broadcasted_iota(jnp.int32, (tk, 1), 0)
        causal = kpos <= t
        new = []
        for g in range(NSA_KV_GROUPS):
            m, l, acc = carry[g]
            gs = slice(g * NSA_DH, (g + 1) * NSA_DH)
            kblk = ks_ref[0, pl.ds(k0, tk), gs]
            vblk = vs_ref[0, pl.ds(k0, tk), gs]
            allowed = (_dot(ext_ref[kt], selt_ref[0, g]) > 0.5) & causal
            bias = jnp.where(allowed, 0.0, MASK_NEG)
            s = _dot_nt(kblk, qs_g[g]) + jnp.concatenate([bias] * NSA_HPG, axis=1)
            m_new = jnp.maximum(m, jnp.max(s, axis=0, keepdims=True))
            alpha = jnp.exp(m - m_new)
            p = jnp.exp(s - m_new)
            l_new = alpha * l + jnp.sum(p, axis=0, keepdims=True)
            new.append((m_new, l_new, alpha * acc + _dot_tn(vblk, p.astype(BF16))))
        return tuple(new)

    init = tuple((jnp.full((1, NSA_HPG * tq), MASK_NEG, F32),
                  jnp.zeros((1, NSA_HPG * tq), F32),
                  jnp.zeros((NSA_DH, NSA_HPG * tq), F32)) for _ in range(NSA_KV_GROUPS))
    sel_state = lax.fori_loop(0, n_kt, body, init)

    outs = []
    for g in range(NSA_KV_GROUPS):
        gs = slice(g * NSA_DH, (g + 1) * NSA_DH)
        qs = qs_g[g]
        _, l, acc = sel_state[g]
        o_s = acc / l

        kwb = kw_ref[0, pl.ds(w0, wlen), gs]
        vwb = vw_ref[0, pl.ds(w0, wlen), gs]
        sw = _dot_nt(kwb, qs) + wbias
        mw = jnp.max(sw, axis=0, keepdims=True)
        pw = jnp.exp(sw - mw)
        lw = jnp.sum(pw, axis=0, keepdims=True)
        o_w = _dot_tn(vwb, pw.astype(BF16)) / lw

        for h in range(NSA_HPG):
            hq = g * NSA_HPG + h
            cs = slice(h * tq, (h + 1) * tq)
            outs.append(gate_t[3 * hq:3 * hq + 1] * oct_ref[0, hq * NSA_DH:(hq + 1) * NSA_DH, :]
                        + gate_t[3 * hq + 1:3 * hq + 2] * o_s[:, cs]
                        + gate_t[3 * hq + 2:3 * hq + 3] * o_w[:, cs])
    o_ref[0] = jnp.concatenate(outs, axis=0).T.astype(o_ref.dtype)


def _sattn(nq, ks, vs, kw, vw, sel_t, oc_t, gate, tq=128, tk=512):
    bsz, seq, width = nq.shape
    nsel = sel_t.shape[2]
    tk = min(tk, seq)
    expand_t = (jnp.arange(seq)[:, None] // SEL_BLOCK == jnp.arange(nsel)[None, :]).astype(BF16)
    expand_t = expand_t.reshape(seq // tk, tk, nsel)
    qblk = pl.BlockSpec((1, tq, width), lambda b, i: (b, i, 0))
    kvblk = pl.BlockSpec((1, seq, LANES), lambda b, i: (b, 0, 0))
    return pl.pallas_call(
        functools.partial(_sattn_kernel, tk),
        grid=(bsz, seq // tq),
        in_specs=[qblk, kvblk, kvblk, kvblk, kvblk,
                  pl.BlockSpec((1, NSA_KV_GROUPS, nsel, tq), lambda b, i: (b, 0, 0, i)),
                  pl.BlockSpec((seq // tk, tk, nsel), lambda b, i: (0, 0, 0)),
                  pl.BlockSpec((1, width, tq), lambda b, i: (b, 0, i)),
                  pl.BlockSpec((1, tq, LANES), lambda b, i: (b, i, 0))],
        out_specs=qblk,
        out_shape=jax.ShapeDtypeStruct((bsz, seq, width), BF16),
        compiler_params=_cparams(("parallel", "parallel")),
        name="sattn",
    )(nq, ks, vs, kw, vw, sel_t, expand_t, oc_t, gate)


_NEXT = PEER_TOPK + 1
_VROWS = 24


def _extract_desc(v, count, rows_out):
    nrow, tm = v.shape
    rid = lax.broadcasted_iota(jnp.int32, (nrow, tm), 0)
    oid = lax.broadcasted_iota(jnp.int32, (rows_out, tm), 0)
    out = jnp.full((rows_out, tm), -jnp.inf, F32)
    for k in range(count):
        m = jnp.max(v, axis=0, keepdims=True)
        idx = jnp.min(jnp.where(v == m, rid, nrow), axis=0, keepdims=True)
        v = jnp.where(rid == idx, -jnp.inf, v)
        out = jnp.where(oid == k, m, out)
    return out


def _pq_kernel(ret_ref, nsa_ref, x_ref, ga_ref, sc_ref, sh_ref, g_ref, wo1_ref, wo2_ref, wq_ref, sk_ref,
               x1_ref, h2_ref, s1_ref, e1_ref, th_ref, e0_ref):
    mix = _dot(ret_ref[0], wo1_ref[...]) + _dot(nsa_ref[0], wo2_ref[...])
    x1 = x_ref[0] + ga_ref[0] * mix
    x1_ref[0] = x1
    h2 = _norm_mod(x1, g_ref[...], sc_ref[0], sh_ref[0]).astype(BF16)
    h2_ref[0] = h2
    qp = _dot(h2, wq_ref[...]).astype(BF16)
    dh = sk_ref.shape[3]
    tm = qp.shape[0]
    rid8 = lax.broadcasted_iota(jnp.int32, (8, tm), 0)
    for p in range(PEER_HEADS):
        s0 = _dot_nt(sk_ref[p, 0], qp[:, (2 * p) * dh:(2 * p + 1) * dh])
        s1 = _dot_nt(sk_ref[p, 1], qp[:, (2 * p + 1) * dh:(2 * p + 2) * dh])
        v0 = _extract_desc(s0, _NEXT, _VROWS)
        v1 = _extract_desc(s1, _NEXT, _VROWS)
        slabs = [v0[0:1] + v1]
        for a in range(1, 8):
            nb = _NEXT // (a + 1)
            slabs.append(jnp.where(rid8 < nb, v0[a:a + 1] + v1[0:8], -jnp.inf))
        slabs.append(v0[8:_VROWS] + v1[0:1])
        cand = jnp.concatenate(slabs, axis=0)
        top = _extract_desc(cand, _NEXT, _VROWS)
        mx = top[0:1]
        kid = lax.broadcasted_iota(jnp.int32, top.shape, 0)
        z = jnp.sum(jnp.where(kid < PEER_TOPK, jnp.exp(top - mx), 0.0), axis=0, keepdims=True)
        tau = 0.5 * (top[PEER_TOPK - 1:PEER_TOPK] + top[PEER_TOPK:PEER_TOPK + 1])
        s1_ref[p] = s1
        e1_ref[p] = jnp.exp(s1 - v1[0:1])
        th_ref[p] = tau - s0
        e0_ref[p] = jnp.exp(s0 - v0[0:1]) / z


def _pq(ret, nsa, x, ga1, sc2, sh2, g, w_out, w_q, subkeys, tm=256):
    bsz, seq, d = x.shape
    tm = min(tm, seq)
    ntok = bsz * seq
    nkeys = subkeys.shape[2]
    nq = w_q.shape[1]
    wr = ret.shape[2]
    wo1 = w_out[:wr].astype(BF16)
    wo2 = w_out[wr:].astype(BF16)
    nt = seq // tm
    tokblk = lambda w: pl.BlockSpec((1, tm, w), lambda b, i: (b, i, 0))
    modblk = pl.BlockSpec((1, 1, d), lambda b, i: (b, 0, 0))
    c2 = lambda b, i: (0, 0)
    tblk = pl.BlockSpec((PEER_HEADS, nkeys, tm), lambda b, i: (0, 0, b * nt + i))
    tsh = jax.ShapeDtypeStruct((PEER_HEADS, nkeys, ntok), F32)
    return pl.pallas_call(
        _pq_kernel,
        grid=(bsz, nt),
        in_specs=[tokblk(wr), tokblk(nsa.shape[2]), tokblk(d), modblk, modblk, modblk,
                  pl.BlockSpec((1, d), c2),
                  pl.BlockSpec(wo1.shape, c2), pl.BlockSpec(wo2.shape, c2),
                  pl.BlockSpec((d, nq), c2, pipeline_mode=pl.Buffered(1)),
                  pl.BlockSpec(subkeys.shape, lambda b, i: (0, 0, 0, 0))],
        out_specs=[tokblk(d), tokblk(d), tblk, tblk, tblk, tblk],
        out_shape=[jax.ShapeDtypeStruct((bsz, seq, d), F32),
                   jax.ShapeDtypeStruct((bsz, seq, d), BF16),
                   tsh, tsh, tsh, tsh],
        compiler_params=_cparams(("parallel", "parallel")),
        name="pq",
    )(ret, nsa, x, ga1.reshape(bsz, 1, d), sc2.reshape(bsz, 1, d), sh2.reshape(bsz, 1, d), g.reshape(1, d),
      wo1, wo2, w_q.astype(BF16), subkeys.astype(BF16))


def _peer_kernel(h_ref, dn_ref, up_ref, s1_ref, e1_ref, th_ref, e0_ref, o_ref, c_ref):
    j = pl.program_id(1)

    @pl.when(j == 0)
    def _():
        o_ref[...] = jnp.zeros_like(o_ref)

    nkeys = s1_ref.shape[1]
    te, tm = c_ref.shape
    a_t = _dot_nt(dn_ref[...], h_ref[...])
    for ii in range(te // nkeys):
        rows = slice(ii * nkeys, (ii + 1) * nkeys)
        w = None
        for p in range(PEER_HEADS):
            contrib = jnp.where(s1_ref[p] >= th_ref[p, ii:ii + 1, :], e1_ref[p], 0.0) * e0_ref[p, ii:ii + 1, :]
            w = contrib if w is None else w + contrib
        c_ref[rows, :] = (w * jax.nn.gelu(a_t[rows])).astype(BF16)
    o_ref[...] += _dot(up_ref[...], c_ref[...])


def _peer(h2, down, up_t, s1, e1, th, e0, tm=512, te=1024):
    ntok, d = h2.shape
    nexp = down.shape[0]
    nkeys = s1.shape[1]
    tm = min(tm, ntok)
    tblk = pl.BlockSpec((PEER_HEADS, nkeys, tm), lambda i, j: (0, 0, i))
    iblk = pl.BlockSpec((PEER_HEADS, te // nkeys, tm), lambda i, j: (0, j, i))
    return pl.pallas_call(
        _peer_kernel,
        grid=(ntok // tm, nexp // te),
        in_specs=[pl.BlockSpec((tm, d), lambda i, j: (i, 0)),
                  pl.BlockSpec((te, d), lambda i, j: (j, 0)),
                  pl.BlockSpec((d, te), lambda i, j: (0, j)),
                  tblk, tblk, iblk, iblk],
        out_specs=pl.BlockSpec((d, tm), lambda i, j: (0, i)),
        out_shape=jax.ShapeDtypeStruct((d, ntok), F32),
        scratch_shapes=[pltpu.VMEM((te, tm), BF16)],
        compiler_params=_cparams(("parallel", "arbitrary")),
        name="peer",
    )(h2, down, up_t, s1, e1, th, e0)


def _final_kernel(normalize, x_ref, p_ref, ga_ref, g_ref, o_ref):
    x = x_ref[0] + ga_ref[0] * p_ref[...].T
    if normalize:
        x = x * lax.rsqrt(jnp.mean(x * x, axis=-1, keepdims=True) + NORM_EPS) * g_ref[...]
    o_ref[0] = x


def _final(x1, peer_t, ga2, g, normalize, tm=256):
    bsz, seq, d = x1.shape
    tm = min(tm, seq)
    nt = seq // tm
    return pl.pallas_call(
        functools.partial(_final_kernel, normalize),
        grid=(bsz, nt),
        in_specs=[pl.BlockSpec((1, tm, d), lambda b, i: (b, i, 0)),
                  pl.BlockSpec((d, tm), lambda b, i: (0, b * nt + i)),
                  pl.BlockSpec((1, 1, d), lambda b, i: (b, 0, 0)),
                  pl.BlockSpec((1, d), lambda b, i: (0, 0))],
        out_specs=pl.BlockSpec((1, tm, d), lambda b, i: (b, i, 0)),
        out_shape=jax.ShapeDtypeStruct((bsz, seq, d), F32),
        compiler_params=_cparams(("parallel", "parallel")),
        name="final",
    )(x1, peer_t, ga2.reshape(bsz, 1, d), g.reshape(1, d))


def _layer(x, mod, g_mix, w_in, pe_k, pe_v, wk1, bk1, wk2, wv1, bv1, wv2, w_out, g_ffn, w_q, subkeys, down, up):
    bsz, seq, d = x.shape
    sh1, sc1, ga1, sh2, sc2, ga2 = jnp.split(mod, 6, axis=-1)
    pr = _inproj(x, sc1, sh1, g_mix, w_in)
    ret = _retention(pr["rq"], pr["rk"], pr["rv"], pr["rg"])
    kcmp, vcmp = _compress(pr["kc"], pr["vc"], pe_k, pe_v, wk1, bk1, wk2, wv1, bv1, wv2)
    oc, sel = _cattn(pr["nq"], kcmp, vcmp)
    nsa = _sattn(pr["nq"], pr["ks"], pr["vs"], pr["kw"], pr["vw"], sel, oc, pr["gate"])
    x1, h2, s1, e1, th, e0 = _pq(ret, nsa, x, ga1, sc2, sh2, g_ffn, w_out, w_q, subkeys)
    peer_t = _peer(h2.reshape(bsz * seq, d), down.astype(BF16), up.T.astype(BF16), s1, e1, th, e0)
    return x1, peer_t, ga2


def kernel(x, c, w_ada, b_ada, g_norm_mix, w_in, pe_cmp_k, pe_cmp_v, w_cmp_k1, b_cmp_k1, w_cmp_k2, w_cmp_v1,
           b_cmp_v1, w_cmp_v2, w_out, g_norm_ffn, w_peer_q, peer_subkeys, peer_down, peer_up, g_norm_final):
    depth = w_ada.shape[0]
    for l in range(depth):
        mod = _ada(c, w_ada[l], b_ada[l])
        x1, peer_t, ga2 = _layer(x, mod, g_norm_mix[l], w_in[l], pe_cmp_k[l], pe_cmp_v[l], w_cmp_k1[l],
                                 b_cmp_k1[l], w_cmp_k2[l], w_cmp_v1[l], b_cmp_v1[l], w_cmp_v2[l], w_out[l],
                                 g_norm_ffn[l], w_peer_q[l], peer_subkeys[l], peer_down[l], peer_up[l])
        x = _final(x1, peer_t, ga2, g_norm_final, normalize=(l == depth - 1))
    return x
```

```python
import functools
import math

import jax
import jax.numpy as jnp
from jax import lax
from jax.experimental import pallas as pl
from jax.experimental.pallas import tpu as pltpu

F32 = jnp.float32
BF16 = jnp.bfloat16

RET_HEADS = 4
RET_DK = 128
RET_DV = 128
RET_CHUNK = 128
NSA_HEADS = 8
NSA_KV_GROUPS = 2
NSA_HPG = NSA_HEADS // NSA_KV_GROUPS
NSA_DH = 64
CMP_LEN = 32
CMP_STRIDE = 16
CMP_HID = 256
SEL_BLOCK = 64
SEL_TOPN = 16
WINDOW = 512
FORCED_SCORE = 1e9
ROPE_THETA = 10000.0
PEER_HEADS = 8
PEER_TOPK = 16
NORM_EPS = 1e-6

LANES = 128
MASK_NEG = -1e30
VMEM_LIMIT = 56 * 1024 * 1024


def _cparams(sem):
    return pltpu.CompilerParams(dimension_semantics=sem, vmem_limit_bytes=VMEM_LIMIT)


def _dot(a, b):
    return jnp.dot(a, b, preferred_element_type=F32)


def _dot_nt(a, b):
    return lax.dot_general(a, b, (((1,), (1,)), ((), ())), preferred_element_type=F32)


def _dot_tn(a, b):
    return lax.dot_general(a, b, (((0,), (0,)), ((), ())), preferred_element_type=F32)


def _split_bf16(a):
    hi = a.astype(BF16)
    lo = (a - hi.astype(F32)).astype(BF16)
    return hi, lo


def _ada_kernel(c_ref, w_ref, b_ref, o_ref):
    c = c_ref[...]
    a = c * jax.nn.sigmoid(c)
    a_hi, a_lo = _split_bf16(a)
    w_hi, w_lo = _split_bf16(w_ref[...])
    o_ref[...] = _dot(a_hi, w_hi) + _dot(a_hi, w_lo) + _dot(a_lo, w_hi) + b_ref[...]


def _ada(c, w, b):
    bsz, d = c.shape
    n = w.shape[1]
    rows = 16
    cp = jnp.zeros((rows, d), F32).at[:bsz].set(c)
    tn = 1024
    out = pl.pallas_call(
        _ada_kernel,
        grid=(n // tn,),
        in_specs=[pl.BlockSpec((rows, d), lambda j: (0, 0)),
                  pl.BlockSpec((d, tn), lambda j: (0, j)),
                  pl.BlockSpec((1, tn), lambda j: (0, j))],
        out_specs=pl.BlockSpec((rows, tn), lambda j: (0, j)),
        out_shape=jax.ShapeDtypeStruct((rows, n), F32),
        compiler_params=_cparams(("parallel",)),
        name="ada",
    )(cp, w, b.reshape(1, n))
    return out[:bsz]


_ROPE_SEGS = (("rq", RET_HEADS * RET_DK, RET_DK, 1.0),
              ("rk", RET_HEADS * RET_DK, RET_DK, RET_DK ** -0.5),
              ("nq", NSA_HEADS * NSA_DH, NSA_DH, NSA_DH ** -0.5),
              ("kc", LANES, NSA_DH, 1.0),
              ("ks", LANES, NSA_DH, 1.0),
              ("kw", LANES, NSA_DH, 1.0))
_PLAIN_SEGS = (("rv", RET_HEADS * RET_DV, BF16),
               ("rg", RET_HEADS * RET_DV, F32),
               ("vc", LANES, BF16),
               ("vs", LANES, BF16),
               ("vw", LANES, BF16),
               ("gate", LANES, F32))


def _norm_mod(x, g, sc, sh):
    y = x * lax.rsqrt(jnp.mean(x * x, axis=-1, keepdims=True) + NORM_EPS)
    return (y * g) * (1.0 + sc) + sh


def _inproj_kernel(x_ref, sc_ref, sh_ref, g_ref, w_ref, wrot_ref, wpl_ref,
                   c128_ref, s128_ref, c64_ref, s64_ref, *out_refs):
    h = _norm_mod(x_ref[0], g_ref[...], sc_ref[0], sh_ref[0])
    hb = h.astype(BF16)
    outs = list(out_refs)
    off = 0
    for (_, width, hd, scale) in _ROPE_SEGS:
        o_ref = outs.pop(0)
        p = _dot(hb, w_ref[:, off:off + width])
        pr = _dot(hb, wrot_ref[:, off:off + width])
        cos = c128_ref[...] if hd == LANES else c64_ref[...]
        sin = s128_ref[...] if hd == LANES else s64_ref[...]
        for cb in range(width // LANES):
            sl = slice(cb * LANES, (cb + 1) * LANES)
            o = p[:, sl] * cos + pr[:, sl] * sin
            if scale != 1.0:
                o = o * scale
            o_ref[0, :, sl] = o.astype(o_ref.dtype)
        off += width
    off = 0
    for (_, width, _) in _PLAIN_SEGS:
        o_ref = outs.pop(0)
        o_ref[0] = _dot(hb, wpl_ref[:, off:off + width]).astype(o_ref.dtype)
        off += width


def _rot_cols(wseg, hd):
    d = wseg.shape[0]
    w3 = wseg.reshape(d, -1, hd)
    half = hd // 2
    return jnp.concatenate([-w3[..., half:], w3[..., :half]], axis=-1).reshape(d, -1)


def _rope_tables(seq, hd):
    pos = jnp.arange(seq)
    inv = ROPE_THETA ** (-jnp.arange(0, hd, 2, dtype=F32) / hd)
    ang = pos.astype(F32)[:, None] * inv[None, :]
    reps = 2 * LANES // hd
    return jnp.tile(jnp.cos(ang), (1, reps)), jnp.tile(jnp.sin(ang), (1, reps))


def _inproj(x, sc, sh, g, w_in, tm=512):
    bsz, seq, d = x.shape
    splits = (RET_HEADS * RET_DK, RET_HEADS * RET_DK, RET_HEADS * RET_DV, RET_HEADS * RET_DV,
              NSA_HEADS * NSA_DH) + (NSA_KV_GROUPS * NSA_DH,) * 6 + (NSA_HEADS * 3,)
    pts = []
    acc = 0
    for s in splits[:-1]:
        acc += s
        pts.append(acc)
    (r_q, r_k, r_v, r_g, n_q, n_kc, n_vc, n_ks, n_vs, n_kw, n_vw, n_gate) = jnp.split(w_in, pts, axis=-1)
    rope_w = {"rq": r_q, "rk": r_k, "nq": n_q, "kc": n_kc, "ks": n_ks, "kw": n_kw}
    w_rope = jnp.concatenate([rope_w[n] for (n, _, _, _) in _ROPE_SEGS], axis=-1).astype(BF16)
    w_rot = jnp.concatenate([_rot_cols(rope_w[n], hd) for (n, _, hd, _) in _ROPE_SEGS], axis=-1).astype(BF16)
    gate_pad = jnp.zeros((d, LANES), F32).at[:, :NSA_HEADS * 3].set(n_gate)
    w_plain = jnp.concatenate([r_v, r_g, n_vc, n_vs, n_vw, gate_pad], axis=-1).astype(BF16)
    c128, s128 = _rope_tables(seq, RET_DK)
    c64, s64 = _rope_tables(seq, NSA_DH)

    tm = min(tm, seq)
    nrope = w_rope.shape[1]
    npl = w_plain.shape[1]
    const2 = lambda b, i: (0, 0)
    tab = pl.BlockSpec((tm, LANES), lambda b, i: (i, 0))
    in_specs = [pl.BlockSpec((1, tm, d), lambda b, i: (b, i, 0)),
                pl.BlockSpec((1, 1, d), lambda b, i: (b, 0, 0)),
                pl.BlockSpec((1, 1, d), lambda b, i: (b, 0, 0)),
                pl.BlockSpec((1, d), const2),
                pl.BlockSpec((d, nrope), const2, pipeline_mode=pl.Buffered(1)),
                pl.BlockSpec((d, nrope), const2, pipeline_mode=pl.Buffered(1)),
                pl.BlockSpec((d, npl), const2, pipeline_mode=pl.Buffered(1)),
                tab, tab, tab, tab]
    out_specs = []
    out_shape = []
    for (_, width, _, _) in _ROPE_SEGS:
        out_specs.append(pl.BlockSpec((1, tm, width), lambda b, i: (b, i, 0)))
        out_shape.append(jax.ShapeDtypeStruct((bsz, seq, width), BF16))
    for (_, width, dt) in _PLAIN_SEGS:
        out_specs.append(pl.BlockSpec((1, tm, width), lambda b, i: (b, i, 0)))
        out_shape.append(jax.ShapeDtypeStruct((bsz, seq, width), dt))
    outs = pl.pallas_call(
        _inproj_kernel,
        grid=(bsz, seq // tm),
        in_specs=in_specs,
        out_specs=out_specs,
        out_shape=out_shape,
        compiler_params=_cparams(("parallel", "parallel")),
        name="inproj",
    )(x, sc.reshape(bsz, 1, d), sh.reshape(bsz, 1, d), g.reshape(1, d), w_rope, w_rot, w_plain,
      c128, s128, c64, s64)
    names = [n for (n, _, _, _) in _ROPE_SEGS] + [n for (n, _, _) in _PLAIN_SEGS]
    return dict(zip(names, outs))


def _ret_kernel(q_ref, k_ref, v_ref, g_ref, dmat_ref, xi_ref, zeta_ref, cd_ref, o_ref, state_ref):
    @pl.when(pl.program_id(1) == 0)
    def _():
        state_ref[...] = jnp.zeros_like(state_ref)

    for h in range(RET_HEADS):
        sl = slice(h * RET_DK, (h + 1) * RET_DK)
        q = q_ref[0, :, sl]
        k = k_ref[0, :, sl]
        v = v_ref[0, :, sl]
        state = state_ref[h]
        scores = _dot_nt(q, k) * dmat_ref[h]
        inner = _dot(scores.astype(BF16), v)
        cross = _dot(q, state.astype(BF16)) * xi_ref[h]
        y = inner + cross
        mu = jnp.mean(y, axis=-1, keepdims=True)
        yc = y - mu
        var = jnp.mean(yc * yc, axis=-1, keepdims=True)
        yn = yc * lax.rsqrt(var + NORM_EPS)
        g = g_ref[0, :, sl]
        o_ref[0, :, sl] = (g * jax.nn.sigmoid(g) * yn).astype(o_ref.dtype)
        kz = (k.astype(F32) * zeta_ref[h]).astype(BF16)
        state_ref[h] = cd_ref[h] * state + _dot_tn(kz, v)


def _retention(q, k, v, g):
    bsz, seq, _ = q.shape
    c = RET_CHUNK
    nh = RET_HEADS
    log_g = jnp.log1p(-jnp.exp2(-5.0 - jnp.arange(nh, dtype=F32)))
    n = jnp.arange(c, dtype=F32)
    diff = n[:, None] - n[None, :]
    causal = diff >= 0
    dmat = jnp.where(causal[None], jnp.exp(jnp.where(causal, diff, 0.0)[None] * log_g[:, None, None]), 0.0)
    zeta = jnp.exp((c - 1 - n)[None, :] * log_g[:, None])
    xi = jnp.exp((n + 1.0)[None, :] * log_g[:, None])
    cdecay = jnp.exp(c * log_g)
    xi_b = jnp.broadcast_to(xi[:, :, None], (nh, c, RET_DV))
    zeta_b = jnp.broadcast_to(zeta[:, :, None], (nh, c, RET_DK))
    cd_b = jnp.broadcast_to(cdecay[:, None, None], (nh, 1, RET_DV))
    width = nh * RET_DK
    blk = pl.BlockSpec((1, c, width), lambda b, i: (b, i, 0))
    const3 = lambda b, i: (0, 0, 0)
    return pl.pallas_call(
        _ret_kernel,
        grid=(bsz, seq // c),
        in_specs=[blk, blk, blk, blk,
                  pl.BlockSpec((nh, c, c), const3),
                  pl.BlockSpec((nh, c, RET_DV), const3),
                  pl.BlockSpec((nh, c, RET_DK), const3),
                  pl.BlockSpec((nh, 1, RET_DV), const3)],
        out_specs=blk,
        out_shape=jax.ShapeDtypeStruct((bsz, seq, width), BF16),
        scratch_shapes=[pltpu.VMEM((nh, RET_DK, RET_DV), F32)],
        compiler_params=_cparams(("parallel", "arbitrary")),
        name="ret",
    )(q, k, v, g, dmat, xi_b, zeta_b, cd_b)


def _cmp_kernel(ncb, xk_ref, xv_ref, wk1_ref, wv1_ref, pk_ref, pv_ref, bk_ref, bv_ref, wk2_ref, wv2_ref,
                ko_ref, vo_ref, shift_ref):
    nrow = xk_ref.shape[1]
    row = lax.broadcasted_iota(jnp.int32, (nrow, 1), 0)
    for (x_ref, w1_ref, p_ref, b_ref, w2_ref, o_ref) in (
            (xk_ref, wk1_ref, pk_ref, bk_ref, wk2_ref, ko_ref),
            (xv_ref, wv1_ref, pv_ref, bv_ref, wv2_ref, vo_ref)):
        a = _dot(x_ref[0], w1_ref[...])
        pw = _dot(p_ref[...], w1_ref[...])
        for g in range(NSA_KV_GROUPS):
            c0 = g * 2 * CMP_HID
            lo = a[:, c0:c0 + CMP_HID]
            hi = a[:, c0 + CMP_HID:c0 + 2 * CMP_HID]
            shift_ref[pl.ds(0, nrow), :] = hi
            shift_ref[pl.ds(nrow, 8), :] = jnp.zeros((8, CMP_HID), F32)
            hi_next = shift_ref[pl.ds(1, nrow), :]
            bias = pw[0:1, c0:c0 + CMP_HID] + pw[1:2, c0 + CMP_HID:c0 + 2 * CMP_HID] + b_ref[...]
            hid = jax.nn.gelu(lo + hi_next + bias)
            out = _dot(hid.astype(BF16), w2_ref[...])
            o_ref[0, g] = jnp.where(row < ncb, out, 0.0).astype(o_ref.dtype)


def _cmp_weights(w1, pe):
    hid = w1.shape[1]
    w1r = w1.reshape(2, CMP_STRIDE, NSA_DH, hid)
    blocks = []
    for g in range(NSA_KV_GROUPS):
        for half in range(2):
            blk = jnp.zeros((CMP_STRIDE, NSA_KV_GROUPS, NSA_DH, hid), F32).at[:, g].set(w1r[half])
            blocks.append(blk.reshape(CMP_STRIDE * NSA_KV_GROUPS * NSA_DH, hid))
    wbig = jnp.concatenate(blocks, axis=-1).astype(BF16)
    per = pe.reshape(2, CMP_STRIDE, 1, NSA_DH)
    pch = jnp.broadcast_to(per, (2, CMP_STRIDE, NSA_KV_GROUPS, NSA_DH)).reshape(2, -1)
    prow = jnp.zeros((16, pch.shape[1]), F32).at[:2].set(pch).astype(BF16)
    return wbig, prow


def _compress(kraw, vraw, pe_k, pe_v, wk1, bk1, wk2, wv1, bv1, wv2):
    bsz, seq, width = kraw.shape
    nrow = seq // CMP_STRIDE
    ncb = (seq - CMP_LEN) // CMP_STRIDE + 1
    xk = kraw.reshape(bsz, nrow, CMP_STRIDE * width)
    xv = vraw.reshape(bsz, nrow, CMP_STRIDE * width)
    wkb, pk = _cmp_weights(wk1, pe_k)
    wvb, pv = _cmp_weights(wv1, pe_v)
    xblk = pl.BlockSpec((1, nrow, CMP_STRIDE * width), lambda b: (b, 0, 0))
    c2 = lambda b: (0, 0)
    oblk = pl.BlockSpec((1, NSA_KV_GROUPS, nrow, NSA_DH), lambda b: (b, 0, 0, 0))
    osh = jax.ShapeDtypeStruct((bsz, NSA_KV_GROUPS, nrow, NSA_DH), BF16)
    return pl.pallas_call(
        functools.partial(_cmp_kernel, ncb),
        grid=(bsz,),
        in_specs=[xblk, xblk,
                  pl.BlockSpec(wkb.shape, c2), pl.BlockSpec(wvb.shape, c2),
                  pl.BlockSpec(pk.shape, c2), pl.BlockSpec(pv.shape, c2),
                  pl.BlockSpec((1, CMP_HID), c2), pl.BlockSpec((1, CMP_HID), c2),
                  pl.BlockSpec((CMP_HID, NSA_DH), c2), pl.BlockSpec((CMP_HID, NSA_DH), c2)],
        out_specs=[oblk, oblk],
        out_shape=[osh, osh],
        scratch_shapes=[pltpu.VMEM((nrow + 8, CMP_HID), F32)],
        compiler_params=_cparams(("parallel",)),
        name="cmp",
    )(xk, xv, wkb, wvb, pk, pv, bk1.reshape(1, -1), bv1.reshape(1, -1), wk2.astype(BF16), wv2.astype(BF16))


def _softmax_cols(s, mask):
    s = jnp.where(mask, s, -jnp.inf)
    m = jnp.max(s, axis=0, keepdims=True)
    m = jnp.where(m > -jnp.inf, m, 0.0)
    e = jnp.where(mask, jnp.exp(s - m), 0.0)
    return e / jnp.maximum(jnp.sum(e, axis=0, keepdims=True), jnp.finfo(F32).tiny)


def _cattn_kernel(q_ref, kc_ref, vc_ref, ovt_ref, oct_ref, selt_ref):
    tq = q_ref.shape[1]
    nrow = kc_ref.shape[2]
    nsel = selt_ref.shape[2]
    n_top = min(SEL_TOPN, nsel)
    q0 = pl.program_id(1) * tq
    t = q0 + lax.broadcasted_iota(jnp.int32, (1, tq), 1)
    cmp_end = lax.broadcasted_iota(jnp.int32, (nrow, 1), 0) * CMP_STRIDE + (CMP_LEN - 1)
    cmask = cmp_end <= t
    blk = lax.broadcasted_iota(jnp.int32, (nsel, tq), 0)
    cur = t // SEL_BLOCK
    forced = (blk == 0) | (blk == cur) | (blk == cur - 1)
    for g in range(NSA_KV_GROUPS):
        kc = kc_ref[0, g]
        vc = vc_ref[0, g]
        imp = jnp.zeros((nsel, tq), F32)
        for h in range(NSA_HPG):
            hq = g * NSA_HPG + h
            sl = slice(hq * NSA_DH, (hq + 1) * NSA_DH)
            p = _softmax_cols(_dot_nt(kc, q_ref[0, :, sl]), cmask)
            pb = p.astype(BF16)
            oct_ref[0, sl, :] = _dot_tn(vc, pb)
            imp = imp + _dot(ovt_ref[...], pb)
        v = jnp.where(forced, FORCED_SCORE, imp)
        v = jnp.where(blk <= cur, v, -jnp.inf)
        sel = jnp.zeros((nsel, tq), F32)
        for _ in range(n_top):
            m = jnp.max(v, axis=0, keepdims=True)
            cand = (v == m) & (m > -jnp.inf)
            idx = jnp.min(jnp.where(cand, blk, nsel), axis=0, keepdims=True)
            pick = blk == idx
            sel = jnp.where(pick, 1.0, sel)
            v = jnp.where(pick, -jnp.inf, v)
        selt_ref[0, g] = sel.astype(selt_ref.dtype)


def _cattn(nq, kcmp, vcmp, tq=256):
    bsz, seq, width = nq.shape
    nrow = kcmp.shape[2]
    nsel = seq // SEL_BLOCK
    tq = min(tq, seq)
    cmp_start = jnp.arange(nrow) * CMP_STRIDE
    sel_start = jnp.arange(nsel) * SEL_BLOCK
    overlap = jnp.clip(jnp.minimum(cmp_start[:, None] + CMP_LEN, sel_start[None, :] + SEL_BLOCK)
                       - jnp.maximum(cmp_start[:, None], sel_start[None, :]), 0).astype(F32) / CMP_STRIDE
    kv = pl.BlockSpec((1, NSA_KV_GROUPS, nrow, NSA_DH), lambda b, i: (b, 0, 0, 0))
    return pl.pallas_call(
        _cattn_kernel,
        grid=(bsz, seq // tq),
        in_specs=[pl.BlockSpec((1, tq, width), lambda b, i: (b, i, 0)), kv, kv,
                  pl.BlockSpec((nsel, nrow), lambda b, i: (0, 0))],
        out_specs=[pl.BlockSpec((1, width, tq), lambda b, i: (b, 0, i)),
                   pl.BlockSpec((1, NSA_KV_GROUPS, nsel, tq), lambda b, i: (b, 0, 0, i))],
        out_shape=[jax.ShapeDtypeStruct((bsz, width, seq), F32),
                   jax.ShapeDtypeStruct((bsz, NSA_KV_GROUPS, nsel, seq), BF16)],
        compiler_params=_cparams(("parallel", "parallel")),
        name="cattn",
    )(nq, kcmp, vcmp, overlap.T.astype(BF16))


def _sattn_kernel(tk, q_ref, ks_ref, vs_ref, kw_ref, vw_ref, selt_ref, ext_ref, oct_ref, gate_ref, o_ref):
    tq = q_ref.shape[1]
    seq = ks_ref.shape[1]
    q0 = pl.program_id(1) * tq
    t = q0 + lax.broadcasted_iota(jnp.int32, (1, tq), 1)
    n_kt = (q0 + tq - 1) // tk + 1
    wlen = min(WINDOW + tq, seq)
    w0 = pl.multiple_of(jnp.maximum(q0 + tq - wlen, 0), 8)
    wpos = w0 + lax.broadcasted_iota(jnp.int32, (wlen, 1), 0)
    wbias = jnp.where((wpos <= t) & (wpos > t - WINDOW), 0.0, MASK_NEG)
    wbias = jnp.concatenate([wbias] * NSA_HPG, axis=1)
    gate_t = jax.nn.sigmoid(gate_ref[0]).T
    qs_g = [jnp.concatenate(
        [q_ref[0, :, (g * NSA_HPG + h) * NSA_DH:(g * NSA_HPG + h + 1) * NSA_DH] for h in range(NSA_HPG)],
        axis=0) for g in range(NSA_KV_GROUPS)]

    def body(kt, carry):
        k0 = pl.multiple_of(kt * tk, tk)
        kpos = k0 + lax.broadcasted_iota(jnp.int32, (tk, 1), 0)
        causal = kpos <= t
        new = []
        for g in range(NSA_KV_GROUPS):
            m, l, acc = carry[g]
            gs = slice(g * NSA_DH, (g + 1) * NSA_DH)
            kblk = ks_ref[0, pl.ds(k0, tk), gs]
            vblk = vs_ref[0, pl.ds(k0, tk), gs]
            allowed = (_dot(ext_ref[kt], selt_ref[0, g]) > 0.5) & causal
            bias = jnp.where(allowed, 0.0, MASK_NEG)
            s = _dot_nt(kblk, qs_g[g]) + jnp.concatenate([bias] * NSA_HPG, axis=1)
            m_new = jnp.maximum(m, jnp.max(s, axis=0, keepdims=True))
            alpha = jnp.exp(m - m_new)
            p = jnp.exp(s - m_new)
            l_new = alpha * l + jnp.sum(p, axis=0, keepdims=True)
            new.append((m_new, l_new, alpha * acc + _dot_tn(vblk, p.astype(BF16))))
        return tuple(new)

    init = tuple((jnp.full((1, NSA_HPG * tq), MASK_NEG, F32),
                  jnp.zeros((1, NSA_HPG * tq), F32),
                  jnp.zeros((NSA_DH, NSA_HPG * tq), F32)) for _ in range(NSA_KV_GROUPS))
    sel_state = lax.fori_loop(0, n_kt, body, init)

    outs = []
    for g in range(NSA_KV_GROUPS):
        gs = slice(g * NSA_DH, (g + 1) * NSA_DH)
        qs = qs_g[g]
        _, l, acc = sel_state[g]
        o_s = acc / l

        kwb = kw_ref[0, pl.ds(w0, wlen), gs]
        vwb = vw_ref[0, pl.ds(w0, wlen), gs]
        sw = _dot_nt(kwb, qs) + wbias
        mw = jnp.max(sw, axis=0, keepdims=True)
        pw = jnp.exp(sw - mw)
        lw = jnp.sum(pw, axis=0, keepdims=True)
        o_w = _dot_tn(vwb, pw.astype(BF16)) / lw

        for h in range(NSA_HPG):
            hq = g * NSA_HPG + h
            cs = slice(h * tq, (h + 1) * tq)
            outs.append(gate_t[3 * hq:3 * hq + 1] * oct_ref[0, hq * NSA_DH:(hq + 1) * NSA_DH, :]
                        + gate_t[3 * hq + 1:3 * hq + 2] * o_s[:, cs]
                        + gate_t[3 * hq + 2:3 * hq + 3] * o_w[:, cs])
    o_ref[0] = jnp.concatenate(outs, axis=0).T.astype(o_ref.dtype)


def _sattn(nq, ks, vs, kw, vw, sel_t, oc_t, gate, tq=128, tk=512):
    bsz, seq, width = nq.shape
    nsel = sel_t.shape[2]
    tk = min(tk, seq)
    expand_t = (jnp.arange(seq)[:, None] // SEL_BLOCK == jnp.arange(nsel)[None, :]).astype(BF16)
    expand_t = expand_t.reshape(seq // tk, tk, nsel)
    qblk = pl.BlockSpec((1, tq, width), lambda b, i: (b, i, 0))
    kvblk = pl.BlockSpec((1, seq, LANES), lambda b, i: (b, 0, 0))
    return pl.pallas_call(
        functools.partial(_sattn_kernel, tk),
        grid=(bsz, seq // tq),
        in_specs=[qblk, kvblk, kvblk, kvblk, kvblk,
                  pl.BlockSpec((1, NSA_KV_GROUPS, nsel, tq), lambda b, i: (b, 0, 0, i)),
                  pl.BlockSpec((seq // tk, tk, nsel), lambda b, i: (0, 0, 0)),
                  pl.BlockSpec((1, width, tq), lambda b, i: (b, 0, i)),
                  pl.BlockSpec((1, tq, LANES), lambda b, i: (b, i, 0))],
        out_specs=qblk,
        out_shape=jax.ShapeDtypeStruct((bsz, seq, width), BF16),
        compiler_params=_cparams(("parallel", "parallel")),
        name="sattn",
    )(nq, ks, vs, kw, vw, sel_t, expand_t, oc_t, gate)


_NEXT = PEER_TOPK + 1
_VROWS = 24


def _extract_desc_ties(v, count, rows_out):
    nrow, tm = v.shape
    rid = lax.broadcasted_iota(jnp.int32, (nrow, tm), 0)
    oid = lax.broadcasted_iota(jnp.int32, (rows_out, tm), 0)
    out = jnp.full((rows_out, tm), -jnp.inf, F32)
    for k in range(count):
        m = jnp.max(v, axis=0, keepdims=True)
        idx = jnp.min(jnp.where(v == m, rid, nrow), axis=0, keepdims=True)
        v = jnp.where(rid == idx, -jnp.inf, v)
        out = jnp.where(oid == k, m, out)
    return out


def _extract_desc(v, count, rows_out):
    _, tm = v.shape
    oid = lax.broadcasted_iota(jnp.int32, (rows_out, tm), 0)
    out = jnp.full((rows_out, tm), -jnp.inf, F32)
    masked_before = jnp.sum(jnp.where(v == -jnp.inf, 1.0, 0.0), axis=0, keepdims=True)
    w = v
    for k in range(count):
        m = jnp.max(w, axis=0, keepdims=True)
        w = jnp.where(w == m, -jnp.inf, w)
        out = jnp.where(oid == k, m, out)
    masked_after = jnp.sum(jnp.where(w == -jnp.inf, 1.0, 0.0), axis=0, keepdims=True)
    has_tie = jnp.max(masked_after - masked_before) > count
    return lax.cond(has_tie, lambda: _extract_desc_ties(v, count, rows_out), lambda: out)


def _pq_kernel(ret_ref, nsa_ref, x_ref, ga_ref, sc_ref, sh_ref, g_ref, wo1_ref, wo2_ref, wq_ref, sk_ref,
               x1_ref, h2_ref, e1_ref, ph_ref, e0_ref):
    mix = _dot(ret_ref[0], wo1_ref[...]) + _dot(nsa_ref[0], wo2_ref[...])
    x1 = x_ref[0] + ga_ref[0] * mix
    x1_ref[0] = x1
    h2 = _norm_mod(x1, g_ref[...], sc_ref[0], sh_ref[0]).astype(BF16)
    h2_ref[0] = h2
    qp = _dot(h2, wq_ref[...]).astype(BF16)
    dh = sk_ref.shape[3]
    tm = qp.shape[0]
    rid8 = lax.broadcasted_iota(jnp.int32, (8, tm), 0)
    for p in range(PEER_HEADS):
        s0 = _dot_nt(sk_ref[p, 0], qp[:, (2 * p) * dh:(2 * p + 1) * dh])
        s1 = _dot_nt(sk_ref[p, 1], qp[:, (2 * p + 1) * dh:(2 * p + 2) * dh])
        v0 = _extract_desc(s0, _NEXT, _VROWS)
        v1 = _extract_desc(s1, _NEXT, _VROWS)
        slabs = [v0[0:1] + v1]
        for a in range(1, 8):
            nb = _NEXT // (a + 1)
            slabs.append(jnp.where(rid8 < nb, v0[a:a + 1] + v1[0:8], -jnp.inf))
        slabs.append(v0[8:_VROWS] + v1[0:1])
        cand = jnp.concatenate(slabs, axis=0)
        top = _extract_desc(cand, _NEXT, _VROWS)
        mx = top[0:1]
        kid = lax.broadcasted_iota(jnp.int32, top.shape, 0)
        z = jnp.sum(jnp.where(kid < PEER_TOPK, jnp.exp(top - mx), 0.0), axis=0, keepdims=True)
        tau = 0.5 * (top[PEER_TOPK - 1:PEER_TOPK] + top[PEER_TOPK:PEER_TOPK + 1])
        e1_ref[p] = jnp.exp(s1 - v1[0:1])
        ph_ref[p] = jnp.exp(tau - s0 - v1[0:1])
        e0_ref[p] = jnp.exp(s0 - v0[0:1]) / z


def _pq(ret, nsa, x, ga1, sc2, sh2, g, w_out, w_q, subkeys, tm=256):
    bsz, seq, d = x.shape
    tm = min(tm, seq)
    ntok = bsz * seq
    nkeys = subkeys.shape[2]
    nq = w_q.shape[1]
    wr = ret.shape[2]
    wo1 = w_out[:wr].astype(BF16)
    wo2 = w_out[wr:].astype(BF16)
    nt = seq // tm
    tokblk = lambda w: pl.BlockSpec((1, tm, w), lambda b, i: (b, i, 0))
    modblk = pl.BlockSpec((1, 1, d), lambda b, i: (b, 0, 0))
    c2 = lambda b, i: (0, 0)
    tblk = pl.BlockSpec((PEER_HEADS, nkeys, tm), lambda b, i: (0, 0, b * nt + i))
    tsh = jax.ShapeDtypeStruct((PEER_HEADS, nkeys, ntok), F32)
    return pl.pallas_call(
        _pq_kernel,
        grid=(bsz, nt),
        in_specs=[tokblk(wr), tokblk(nsa.shape[2]), tokblk(d), modblk, modblk, modblk,
                  pl.BlockSpec((1, d), c2),
                  pl.BlockSpec(wo1.shape, c2), pl.BlockSpec(wo2.shape, c2),
                  pl.BlockSpec((d, nq), c2, pipeline_mode=pl.Buffered(1)),
                  pl.BlockSpec(subkeys.shape, lambda b, i: (0, 0, 0, 0))],
        out_specs=[tokblk(d), tokblk(d), tblk, tblk, tblk],
        out_shape=[jax.ShapeDtypeStruct((bsz, seq, d), F32),
                   jax.ShapeDtypeStruct((bsz, seq, d), BF16),
                   tsh, tsh, tsh],
        compiler_params=_cparams(("parallel", "parallel")),
        name="pq",
    )(ret, nsa, x, ga1.reshape(bsz, 1, d), sc2.reshape(bsz, 1, d), sh2.reshape(bsz, 1, d), g.reshape(1, d),
      wo1, wo2, w_q.astype(BF16), subkeys.astype(BF16))


def _peer_kernel(h_ref, dn_ref, up_ref, e1_ref, ph_ref, e0_ref, o_ref, c_ref):
    j = pl.program_id(1)

    @pl.when(j == 0)
    def _():
        o_ref[...] = jnp.zeros_like(o_ref)

    nkeys = e1_ref.shape[1]
    te = dn_ref.shape[0]
    a_t = _dot_nt(dn_ref[...], h_ref[...])
    for ii in range(te // nkeys):
        rows = slice(ii * nkeys, (ii + 1) * nkeys)
        w = None
        for p in range(PEER_HEADS):
            e1 = e1_ref[p]
            contrib = jnp.where(e1 >= ph_ref[p, ii:ii + 1, :], e1, 0.0) * e0_ref[p, ii:ii + 1, :]
            w = contrib if w is None else w + contrib
        c_ref[rows, :] = (w * jax.nn.gelu(a_t[rows])).astype(BF16)
    o_ref[...] += _dot(up_ref[...], c_ref[...])


def _peer(h2, down, up_t, e1, ph, e0, tm=512, te=1024):
    ntok, d = h2.shape
    nexp = down.shape[0]
    nkeys = e1.shape[1]
    tm = min(tm, ntok)
    tblk = pl.BlockSpec((PEER_HEADS, nkeys, tm), lambda i, j: (0, 0, i))
    iblk = pl.BlockSpec((PEER_HEADS, te // nkeys, tm), lambda i, j: (0, j, i))
    return pl.pallas_call(
        _peer_kernel,
        grid=(ntok // tm, nexp // te),
        in_specs=[pl.BlockSpec((tm, d), lambda i, j: (i, 0)),
                  pl.BlockSpec((te, d), lambda i, j: (j, 0)),
                  pl.BlockSpec((d, te), lambda i, j: (0, j)),
                  tblk, iblk, iblk],
        out_specs=pl.BlockSpec((d, tm), lambda i, j: (0, i)),
        out_shape=jax.ShapeDtypeStruct((d, ntok), F32),
        scratch_shapes=[pltpu.VMEM((te, tm), BF16)],
        compiler_params=_cparams(("parallel", "arbitrary")),
        name="peer",
    )(h2, down, up_t, e1, ph, e0)


def _final_kernel(normalize, x_ref, p_ref, ga_ref, g_ref, o_ref):
    x = x_ref[0] + ga_ref[0] * p_ref[...].T
    if normalize:
        x = x * lax.rsqrt(jnp.mean(x * x, axis=-1, keepdims=True) + NORM_EPS) * g_ref[...]
    o_ref[0] = x


def _final(x1, peer_t, ga2, g, normalize, tm=256):
    bsz, seq, d = x1.shape
    tm = min(tm, seq)
    nt = seq // tm
    return pl.pallas_call(
        functools.partial(_final_kernel, normalize),
        grid=(bsz, nt),
        in_specs=[pl.BlockSpec((1, tm, d), lambda b, i: (b, i, 0)),
                  pl.BlockSpec((d, tm), lambda b, i: (0, b * nt + i)),
                  pl.BlockSpec((1, 1, d), lambda b, i: (b, 0, 0)),
                  pl.BlockSpec((1, d), lambda b, i: (0, 0))],
        out_specs=pl.BlockSpec((1, tm, d), lambda b, i: (b, i, 0)),
        out_shape=jax.ShapeDtypeStruct((bsz, seq, d), F32),
        compiler_params=_cparams(("parallel", "parallel")),
        name="final",
    )(x1, peer_t, ga2.reshape(bsz, 1, d), g.reshape(1, d))


def _layer(x, mod, g_mix, w_in, pe_k, pe_v, wk1, bk1, wk2, wv1, bv1, wv2, w_out, g_ffn, w_q, subkeys, down, up):
    bsz, seq, d = x.shape
    sh1, sc1, ga1, sh2, sc2, ga2 = jnp.split(mod, 6, axis=-1)
    pr = _inproj(x, sc1, sh1, g_mix, w_in)
    ret = _retention(pr["rq"], pr["rk"], pr["rv"], pr["rg"])
    kcmp, vcmp = _compress(pr["kc"], pr["vc"], pe_k, pe_v, wk1, bk1, wk2, wv1, bv1, wv2)
    oc, sel = _cattn(pr["nq"], kcmp, vcmp)
    nsa = _sattn(pr["nq"], pr["ks"], pr["vs"], pr["kw"], pr["vw"], sel, oc, pr["gate"])
    x1, h2, e1, ph, e0 = _pq(ret, nsa, x, ga1, sc2, sh2, g_ffn, w_out, w_q, subkeys)
    peer_t = _peer(h2.reshape(bsz * seq, d), down.astype(BF16), up.T.astype(BF16), e1, ph, e0)
    return x1, peer_t, ga2


def kernel(x, c, w_ada, b_ada, g_norm_mix, w_in, pe_cmp_k, pe_cmp_v, w_cmp_k1, b_cmp_k1, w_cmp_k2, w_cmp_v1,
           b_cmp_v1, w_cmp_v2, w_out, g_norm_ffn, w_peer_q, peer_subkeys, peer_down, peer_up, g_norm_final):
    depth = w_ada.shape[0]
    for l in range(depth):
        mod = _ada(c, w_ada[l], b_ada[l])
        x1, peer_t, ga2 = _layer(x, mod, g_norm_mix[l], w_in[l], pe_cmp_k[l], pe_cmp_v[l], w_cmp_k1[l],
                                 b_cmp_k1[l], w_cmp_k2[l], w_cmp_v1[l], b_cmp_v1[l], w_cmp_v2[l], w_out[l],
                                 g_norm_ffn[l], w_peer_q[l], peer_subkeys[l], peer_down[l], peer_up[l])
        x = _final(x1, peer_t, ga2, g_norm_final, normalize=(l == depth - 1))
    return x
```

```python
import functools
import math

import jax
import jax.numpy as jnp
from jax import lax
from jax.experimental import pallas as pl
from jax.experimental.pallas import tpu as pltpu

F32 = jnp.float32
BF16 = jnp.bfloat16

RET_HEADS = 4
RET_DK = 128
RET_DV = 128
RET_CHUNK = 128
NSA_HEADS = 8
NSA_KV_GROUPS = 2
NSA_HPG = NSA_HEADS // NSA_KV_GROUPS
NSA_DH = 64
CMP_LEN = 32
CMP_STRIDE = 16
CMP_HID = 256
SEL_BLOCK = 64
SEL_TOPN = 16
WINDOW = 512
FORCED_SCORE = 1e9
ROPE_THETA = 10000.0
PEER_HEADS = 8
PEER_TOPK = 16
NORM_EPS = 1e-6

LANES = 128
MASK_NEG = -1e30
VMEM_LIMIT = 56 * 1024 * 1024


def _cparams(sem):
    return pltpu.CompilerParams(dimension_semantics=sem, vmem_limit_bytes=VMEM_LIMIT)


def _dot(a, b):
    return jnp.dot(a, b, preferred_element_type=F32)


def _dot_nt(a, b):
    return lax.dot_general(a, b, (((1,), (1,)), ((), ())), preferred_element_type=F32)


def _dot_tn(a, b):
    return lax.dot_general(a, b, (((0,), (0,)), ((), ())), preferred_element_type=F32)


def _split_bf16(a):
    hi = a.astype(BF16)
    lo = (a - hi.astype(F32)).astype(BF16)
    return hi, lo


def _ada_kernel(c_ref, w_ref, b_ref, o_ref):
    c = c_ref[...]
    a = c * jax.nn.sigmoid(c)
    a_hi, a_lo = _split_bf16(a)
    w_hi, w_lo = _split_bf16(w_ref[...])
    o_ref[...] = _dot(a_hi, w_hi) + _dot(a_hi, w_lo) + _dot(a_lo, w_hi) + b_ref[...]


def _ada(c, w, b):
    bsz, d = c.shape
    n = w.shape[1]
    rows = 16
    cp = jnp.zeros((rows, d), F32).at[:bsz].set(c)
    tn = 1024
    out = pl.pallas_call(
        _ada_kernel,
        grid=(n // tn,),
        in_specs=[pl.BlockSpec((rows, d), lambda j: (0, 0)),
                  pl.BlockSpec((d, tn), lambda j: (0, j)),
                  pl.BlockSpec((1, tn), lambda j: (0, j))],
        out_specs=pl.BlockSpec((rows, tn), lambda j: (0, j)),
        out_shape=jax.ShapeDtypeStruct((rows, n), F32),
        compiler_params=_cparams(("parallel",)),
        name="ada",
    )(cp, w, b.reshape(1, n))
    return out[:bsz]


_ROPE_SEGS = (("rq", RET_HEADS * RET_DK, RET_DK, 1.0),
              ("rk", RET_HEADS * RET_DK, RET_DK, RET_DK ** -0.5),
              ("nq", NSA_HEADS * NSA_DH, NSA_DH, NSA_DH ** -0.5),
              ("kc", LANES, NSA_DH, 1.0),
              ("ks", LANES, NSA_DH, 1.0),
              ("kw", LANES, NSA_DH, 1.0))
_PLAIN_SEGS = (("rv", RET_HEADS * RET_DV, BF16),
               ("rg", RET_HEADS * RET_DV, F32),
               ("vc", LANES, BF16),
               ("vs", LANES, BF16),
               ("vw", LANES, BF16),
               ("gate", LANES, F32))


def _norm_mod(x, g, sc, sh):
    y = x * lax.rsqrt(jnp.mean(x * x, axis=-1, keepdims=True) + NORM_EPS)
    return (y * g) * (1.0 + sc) + sh


def _inproj_kernel(x_ref, sc_ref, sh_ref, g_ref, w_ref, wrot_ref, wpl_ref,
                   c128_ref, s128_ref, c64_ref, s64_ref, *out_refs):
    h = _norm_mod(x_ref[0], g_ref[...], sc_ref[0], sh_ref[0])
    hb = h.astype(BF16)
    outs = list(out_refs)
    off = 0
    for (_, width, hd, scale) in _ROPE_SEGS:
        o_ref = outs.pop(0)
        p = _dot(hb, w_ref[:, off:off + width])
        pr = _dot(hb, wrot_ref[:, off:off + width])
        cos = c128_ref[...] if hd == LANES else c64_ref[...]
        sin = s128_ref[...] if hd == LANES else s64_ref[...]
        for cb in range(width // LANES):
            sl = slice(cb * LANES, (cb + 1) * LANES)
            o = p[:, sl] * cos + pr[:, sl] * sin
            if scale != 1.0:
                o = o * scale
            o_ref[0, :, sl] = o.astype(o_ref.dtype)
        off += width
    off = 0
    for (_, width, _) in _PLAIN_SEGS:
        o_ref = outs.pop(0)
        o_ref[0] = _dot(hb, wpl_ref[:, off:off + width]).astype(o_ref.dtype)
        off += width


def _rot_cols(wseg, hd):
    d = wseg.shape[0]
    w3 = wseg.reshape(d, -1, hd)
    half = hd // 2
    return jnp.concatenate([-w3[..., half:], w3[..., :half]], axis=-1).reshape(d, -1)


def _rope_tables(seq, hd):
    pos = jnp.arange(seq)
    inv = ROPE_THETA ** (-jnp.arange(0, hd, 2, dtype=F32) / hd)
    ang = pos.astype(F32)[:, None] * inv[None, :]
    reps = 2 * LANES // hd
    return jnp.tile(jnp.cos(ang), (1, reps)), jnp.tile(jnp.sin(ang), (1, reps))


def _inproj(x, sc, sh, g, w_in, tm=512):
    bsz, seq, d = x.shape
    splits = (RET_HEADS * RET_DK, RET_HEADS * RET_DK, RET_HEADS * RET_DV, RET_HEADS * RET_DV,
              NSA_HEADS * NSA_DH) + (NSA_KV_GROUPS * NSA_DH,) * 6 + (NSA_HEADS * 3,)
    pts = []
    acc = 0
    for s in splits[:-1]:
        acc += s
        pts.append(acc)
    (r_q, r_k, r_v, r_g, n_q, n_kc, n_vc, n_ks, n_vs, n_kw, n_vw, n_gate) = jnp.split(w_in, pts, axis=-1)
    rope_w = {"rq": r_q, "rk": r_k, "nq": n_q, "kc": n_kc, "ks": n_ks, "kw": n_kw}
    w_rope = jnp.concatenate([rope_w[n] for (n, _, _, _) in _ROPE_SEGS], axis=-1).astype(BF16)
    w_rot = jnp.concatenate([_rot_cols(rope_w[n], hd) for (n, _, hd, _) in _ROPE_SEGS], axis=-1).astype(BF16)
    gate_pad = jnp.zeros((d, LANES), F32).at[:, :NSA_HEADS * 3].set(n_gate)
    w_plain = jnp.concatenate([r_v, r_g, n_vc, n_vs, n_vw, gate_pad], axis=-1).astype(BF16)
    c128, s128 = _rope_tables(seq, RET_DK)
    c64, s64 = _rope_tables(seq, NSA_DH)

    tm = min(tm, seq)
    nrope = w_rope.shape[1]
    npl = w_plain.shape[1]
    const2 = lambda b, i: (0, 0)
    tab = pl.BlockSpec((tm, LANES), lambda b, i: (i, 0))
    in_specs = [pl.BlockSpec((1, tm, d), lambda b, i: (b, i, 0)),
                pl.BlockSpec((1, 1, d), lambda b, i: (b, 0, 0)),
                pl.BlockSpec((1, 1, d), lambda b, i: (b, 0, 0)),
                pl.BlockSpec((1, d), const2),
                pl.BlockSpec((d, nrope), const2, pipeline_mode=pl.Buffered(1)),
                pl.BlockSpec((d, nrope), const2, pipeline_mode=pl.Buffered(1)),
                pl.BlockSpec((d, npl), const2, pipeline_mode=pl.Buffered(1)),
                tab, tab, tab, tab]
    out_specs = []
    out_shape = []
    for (_, width, _, _) in _ROPE_SEGS:
        out_specs.append(pl.BlockSpec((1, tm, width), lambda b, i: (b, i, 0)))
        out_shape.append(jax.ShapeDtypeStruct((bsz, seq, width), BF16))
    for (_, width, dt) in _PLAIN_SEGS:
        out_specs.append(pl.BlockSpec((1, tm, width), lambda b, i: (b, i, 0)))
        out_shape.append(jax.ShapeDtypeStruct((bsz, seq, width), dt))
    outs = pl.pallas_call(
        _inproj_kernel,
        grid=(bsz, seq // tm),
        in_specs=in_specs,
        out_specs=out_specs,
        out_shape=out_shape,
        compiler_params=_cparams(("parallel", "parallel")),
        name="inproj",
    )(x, sc.reshape(bsz, 1, d), sh.reshape(bsz, 1, d), g.reshape(1, d), w_rope, w_rot, w_plain,
      c128, s128, c64, s64)
    names = [n for (n, _, _, _) in _ROPE_SEGS] + [n for (n, _, _) in _PLAIN_SEGS]
    return dict(zip(names, outs))


def _ret_kernel(q_ref, k_ref, v_ref, g_ref, dmat_ref, xi_ref, zeta_ref, cd_ref, o_ref, state_ref):
    @pl.when(pl.program_id(1) == 0)
    def _():
        state_ref[...] = jnp.zeros_like(state_ref)

    for h in range(RET_HEADS):
        sl = slice(h * RET_DK, (h + 1) * RET_DK)
        q = q_ref[0, :, sl]
        k = k_ref[0, :, sl]
        v = v_ref[0, :, sl]
        state = state_ref[h]
        scores = _dot_nt(q, k) * dmat_ref[h]
        inner = _dot(scores.astype(BF16), v)
        cross = _dot(q, state.astype(BF16)) * xi_ref[h]
        y = inner + cross
        mu = jnp.mean(y, axis=-1, keepdims=True)
        yc = y - mu
        var = jnp.mean(yc * yc, axis=-1, keepdims=True)
        yn = yc * lax.rsqrt(var + NORM_EPS)
        g = g_ref[0, :, sl]
        o_ref[0, :, sl] = (g * jax.nn.sigmoid(g) * yn).astype(o_ref.dtype)
        kz = (k.astype(F32) * zeta_ref[h]).astype(BF16)
        state_ref[h] = cd_ref[h] * state + _dot_tn(kz, v)


def _retention(q, k, v, g):
    bsz, seq, _ = q.shape
    c = RET_CHUNK
    nh = RET_HEADS
    log_g = jnp.log1p(-jnp.exp2(-5.0 - jnp.arange(nh, dtype=F32)))
    n = jnp.arange(c, dtype=F32)
    diff = n[:, None] - n[None, :]
    causal = diff >= 0
    dmat = jnp.where(causal[None], jnp.exp(jnp.where(causal, diff, 0.0)[None] * log_g[:, None, None]), 0.0)
    zeta = jnp.exp((c - 1 - n)[None, :] * log_g[:, None])
    xi = jnp.exp((n + 1.0)[None, :] * log_g[:, None])
    cdecay = jnp.exp(c * log_g)
    xi_b = jnp.broadcast_to(xi[:, :, None], (nh, c, RET_DV))
    zeta_b = jnp.broadcast_to(zeta[:, :, None], (nh, c, RET_DK))
    cd_b = jnp.broadcast_to(cdecay[:, None, None], (nh, 1, RET_DV))
    width = nh * RET_DK
    blk = pl.BlockSpec((1, c, width), lambda b, i: (b, i, 0))
    const3 = lambda b, i: (0, 0, 0)
    return pl.pallas_call(
        _ret_kernel,
        grid=(bsz, seq // c),
        in_specs=[blk, blk, blk, blk,
                  pl.BlockSpec((nh, c, c), const3),
                  pl.BlockSpec((nh, c, RET_DV), const3),
                  pl.BlockSpec((nh, c, RET_DK), const3),
                  pl.BlockSpec((nh, 1, RET_DV), const3)],
        out_specs=blk,
        out_shape=jax.ShapeDtypeStruct((bsz, seq, width), BF16),
        scratch_shapes=[pltpu.VMEM((nh, RET_DK, RET_DV), F32)],
        compiler_params=_cparams(("parallel", "arbitrary")),
        name="ret",
    )(q, k, v, g, dmat, xi_b, zeta_b, cd_b)


def _cmp_kernel(ncb, xk_ref, xv_ref, wk1_ref, wv1_ref, pk_ref, pv_ref, bk_ref, bv_ref, wk2_ref, wv2_ref,
                ko_ref, vo_ref, shift_ref):
    nrow = xk_ref.shape[1]
    row = lax.broadcasted_iota(jnp.int32, (nrow, 1), 0)
    for (x_ref, w1_ref, p_ref, b_ref, w2_ref, o_ref) in (
            (xk_ref, wk1_ref, pk_ref, bk_ref, wk2_ref, ko_ref),
            (xv_ref, wv1_ref, pv_ref, bv_ref, wv2_ref, vo_ref)):
        a = _dot(x_ref[0], w1_ref[...])
        pw = _dot(p_ref[...], w1_ref[...])
        for g in range(NSA_KV_GROUPS):
            c0 = g * 2 * CMP_HID
            lo = a[:, c0:c0 + CMP_HID]
            hi = a[:, c0 + CMP_HID:c0 + 2 * CMP_HID]
            shift_ref[pl.ds(0, nrow), :] = hi
            shift_ref[pl.ds(nrow, 8), :] = jnp.zeros((8, CMP_HID), F32)
            hi_next = shift_ref[pl.ds(1, nrow), :]
            bias = pw[0:1, c0:c0 + CMP_HID] + pw[1:2, c0 + CMP_HID:c0 + 2 * CMP_HID] + b_ref[...]
            hid = jax.nn.gelu(lo + hi_next + bias)
            out = _dot(hid.astype(BF16), w2_ref[...])
            o_ref[0, g] = jnp.where(row < ncb, out, 0.0).astype(o_ref.dtype)


def _cmp_weights(w1, pe):
    hid = w1.shape[1]
    w1r = w1.reshape(2, CMP_STRIDE, NSA_DH, hid)
    blocks = []
    for g in range(NSA_KV_GROUPS):
        for half in range(2):
            blk = jnp.zeros((CMP_STRIDE, NSA_KV_GROUPS, NSA_DH, hid), F32).at[:, g].set(w1r[half])
            blocks.append(blk.reshape(CMP_STRIDE * NSA_KV_GROUPS * NSA_DH, hid))
    wbig = jnp.concatenate(blocks, axis=-1).astype(BF16)
    per = pe.reshape(2, CMP_STRIDE, 1, NSA_DH)
    pch = jnp.broadcast_to(per, (2, CMP_STRIDE, NSA_KV_GROUPS, NSA_DH)).reshape(2, -1)
    prow = jnp.zeros((16, pch.shape[1]), F32).at[:2].set(pch).astype(BF16)
    return wbig, prow


def _compress(kraw, vraw, pe_k, pe_v, wk1, bk1, wk2, wv1, bv1, wv2):
    bsz, seq, width = kraw.shape
    nrow = seq // CMP_STRIDE
    ncb = (seq - CMP_LEN) // CMP_STRIDE + 1
    xk = kraw.reshape(bsz, nrow, CMP_STRIDE * width)
    xv = vraw.reshape(bsz, nrow, CMP_STRIDE * width)
    wkb, pk = _cmp_weights(wk1, pe_k)
    wvb, pv = _cmp_weights(wv1, pe_v)
    xblk = pl.BlockSpec((1, nrow, CMP_STRIDE * width), lambda b: (b, 0, 0))
    c2 = lambda b: (0, 0)
    oblk = pl.BlockSpec((1, NSA_KV_GROUPS, nrow, NSA_DH), lambda b: (b, 0, 0, 0))
    osh = jax.ShapeDtypeStruct((bsz, NSA_KV_GROUPS, nrow, NSA_DH), BF16)
    return pl.pallas_call(
        functools.partial(_cmp_kernel, ncb),
        grid=(bsz,),
        in_specs=[xblk, xblk,
                  pl.BlockSpec(wkb.shape, c2), pl.BlockSpec(wvb.shape, c2),
                  pl.BlockSpec(pk.shape, c2), pl.BlockSpec(pv.shape, c2),
                  pl.BlockSpec((1, CMP_HID), c2), pl.BlockSpec((1, CMP_HID), c2),
                  pl.BlockSpec((CMP_HID, NSA_DH), c2), pl.BlockSpec((CMP_HID, NSA_DH), c2)],
        out_specs=[oblk, oblk],
        out_shape=[osh, osh],
        scratch_shapes=[pltpu.VMEM((nrow + 8, CMP_HID), F32)],
        compiler_params=_cparams(("parallel",)),
        name="cmp",
    )(xk, xv, wkb, wvb, pk, pv, bk1.reshape(1, -1), bv1.reshape(1, -1), wk2.astype(BF16), wv2.astype(BF16))


def _softmax_cols(s, mask):
    s = jnp.where(mask, s, -jnp.inf)
    m = jnp.max(s, axis=0, keepdims=True)
    m = jnp.where(m > -jnp.inf, m, 0.0)
    e = jnp.where(mask, jnp.exp(s - m), 0.0)
    return e / jnp.maximum(jnp.sum(e, axis=0, keepdims=True), jnp.finfo(F32).tiny)


def _cattn_kernel(q_ref, kc_ref, vc_ref, ovt_ref, oct_ref, selt_ref):
    tq = q_ref.shape[1]
    nrow = kc_ref.shape[2]
    nsel = selt_ref.shape[2]
    n_top = min(SEL_TOPN, nsel)
    q0 = pl.program_id(1) * tq
    t = q0 + lax.broadcasted_iota(jnp.int32, (1, tq), 1)
    cmp_end = lax.broadcasted_iota(jnp.int32, (nrow, 1), 0) * CMP_STRIDE + (CMP_LEN - 1)
    cmask = cmp_end <= t
    blk = lax.broadcasted_iota(jnp.int32, (nsel, tq), 0)
    cur = t // SEL_BLOCK
    forced = (blk == 0) | (blk == cur) | (blk == cur - 1)
    for g in range(NSA_KV_GROUPS):
        kc = kc_ref[0, g]
        vc = vc_ref[0, g]
        imp = jnp.zeros((nsel, tq), F32)
        for h in range(NSA_HPG):
            hq = g * NSA_HPG + h
            sl = slice(hq * NSA_DH, (hq + 1) * NSA_DH)
            p = _softmax_cols(_dot_nt(kc, q_ref[0, :, sl]), cmask)
            pb = p.astype(BF16)
            oct_ref[0, sl, :] = _dot_tn(vc, pb)
            imp = imp + _dot(ovt_ref[...], pb)
        v = jnp.where(forced, FORCED_SCORE, imp)
        v = jnp.where(blk <= cur, v, -jnp.inf)
        sel = jnp.zeros((nsel, tq), F32)
        for _ in range(n_top):
            m = jnp.max(v, axis=0, keepdims=True)
            cand = (v == m) & (m > -jnp.inf)
            idx = jnp.min(jnp.where(cand, blk, nsel), axis=0, keepdims=True)
            pick = blk == idx
            sel = jnp.where(pick, 1.0, sel)
            v = jnp.where(pick, -jnp.inf, v)
        selt_ref[0, g] = sel.astype(selt_ref.dtype)


def _cattn(nq, kcmp, vcmp, tq=256):
    bsz, seq, width = nq.shape
    nrow = kcmp.shape[2]
    nsel = seq // SEL_BLOCK
    tq = min(tq, seq)
    cmp_start = jnp.arange(nrow) * CMP_STRIDE
    sel_start = jnp.arange(nsel) * SEL_BLOCK
    overlap = jnp.clip(jnp.minimum(cmp_start[:, None] + CMP_LEN, sel_start[None, :] + SEL_BLOCK)
                       - jnp.maximum(cmp_start[:, None], sel_start[None, :]), 0).astype(F32) / CMP_STRIDE
    kv = pl.BlockSpec((1, NSA_KV_GROUPS, nrow, NSA_DH), lambda b, i: (b, 0, 0, 0))
    return pl.pallas_call(
        _cattn_kernel,
        grid=(bsz, seq // tq),
        in_specs=[pl.BlockSpec((1, tq, width), lambda b, i: (b, i, 0)), kv, kv,
                  pl.BlockSpec((nsel, nrow), lambda b, i: (0, 0))],
        out_specs=[pl.BlockSpec((1, width, tq), lambda b, i: (b, 0, i)),
                   pl.BlockSpec((1, NSA_KV_GROUPS, nsel, tq), lambda b, i: (b, 0, 0, i))],
        out_shape=[jax.ShapeDtypeStruct((bsz, width, seq), F32),
                   jax.ShapeDtypeStruct((bsz, NSA_KV_GROUPS, nsel, seq), BF16)],
        compiler_params=_cparams(("parallel", "parallel")),
        name="cattn",
    )(nq, kcmp, vcmp, overlap.T.astype(BF16))


def _sattn_kernel(tk, q_ref, ks_ref, vs_ref, kw_ref, vw_ref, selt_ref, ext_ref, oct_ref, gate_ref, o_ref):
    tq = q_ref.shape[1]
    seq = ks_ref.shape[1]
    q0 = pl.program_id(1) * tq
    t = q0 + lax.broadcasted_iota(jnp.int32, (1, tq), 1)
    n_kt = (q0 + tq - 1) // tk + 1
    wlen = min(WINDOW + tq, seq)
    w0 = pl.multiple_of(jnp.maximum(q0 + tq - wlen, 0), 8)
    wpos = w0 + lax.broadcasted_iota(jnp.int32, (wlen, 1), 0)
    wbias = jnp.where((wpos <= t) & (wpos > t - WINDOW), 0.0, MASK_NEG)
    wbias = jnp.concatenate([wbias] * NSA_HPG, axis=1)
    gate_t = jax.nn.sigmoid(gate_ref[0]).T
    qs_g = [jnp.concatenate(
        [q_ref[0, :, (g * NSA_HPG + h) * NSA_DH:(g * NSA_HPG + h + 1) * NSA_DH] for h in range(NSA_HPG)],
        axis=0) for g in range(NSA_KV_GROUPS)]

    def body(kt, carry):
        k0 = pl.multiple_of(kt * tk, tk)
        kpos = k0 + lax.broadcasted_iota(jnp.int32, (tk, 1), 0)
        causal = kpos <= t
        new = []
        for g in range(NSA_KV_GROUPS):
            m, l, acc = carry[g]
            gs = slice(g * NSA_DH, (g + 1) * NSA_DH)
            kblk = ks_ref[0, pl.ds(k0, tk), gs]
            vblk = vs_ref[0, pl.ds(k0, tk), gs]
            allowed = (_dot(ext_ref[kt], selt_ref[0, g]) > 0.5) & causal
            bias = jnp.where(allowed, 0.0, MASK_NEG)
            s = _dot_nt(kblk, qs_g[g]) + jnp.concatenate([bias] * NSA_HPG, axis=1)
            m_new = jnp.maximum(m, jnp.max(s, axis=0, keepdims=True))
            alpha = jnp.exp(m - m_new)
            p = jnp.exp(s - m_new)
            l_new = alpha * l + jnp.sum(p, axis=0, keepdims=True)
            new.append((m_new, l_new, alpha * acc + _dot_tn(vblk, p.astype(BF16))))
        return tuple(new)

    init = tuple((jnp.full((1, NSA_HPG * tq), MASK_NEG, F32),
                  jnp.zeros((1, NSA_HPG * tq), F32),
                  jnp.zeros((NSA_DH, NSA_HPG * tq), F32)) for _ in range(NSA_KV_GROUPS))
    sel_state = lax.fori_loop(0, n_kt, body, init)

    outs = []
    for g in range(NSA_KV_GROUPS):
        gs = slice(g * NSA_DH, (g + 1) * NSA_DH)
        qs = qs_g[g]
        _, l, acc = sel_state[g]
        o_s = acc / l

        kwb = kw_ref[0, pl.ds(w0, wlen), gs]
        vwb = vw_ref[0, pl.ds(w0, wlen), gs]
        sw = _dot_nt(kwb, qs) + wbias
        mw = jnp.max(sw, axis=0, keepdims=True)
        pw = jnp.exp(sw - mw)
        lw = jnp.sum(pw, axis=0, keepdims=True)
        o_w = _dot_tn(vwb, pw.astype(BF16)) / lw

        for h in range(NSA_HPG):
            hq = g * NSA_HPG + h
            cs = slice(h * tq, (h + 1) * tq)
            outs.append(gate_t[3 * hq:3 * hq + 1] * oct_ref[0, hq * NSA_DH:(hq + 1) * NSA_DH, :]
                        + gate_t[3 * hq + 1:3 * hq + 2] * o_s[:, cs]
                        + gate_t[3 * hq + 2:3 * hq + 3] * o_w[:, cs])
    o_ref[0] = jnp.concatenate(outs, axis=0).T.astype(o_ref.dtype)


def _sattn(nq, ks, vs, kw, vw, sel_t, oc_t, gate, tq=128, tk=512):
    bsz, seq, width = nq.shape
    nsel = sel_t.shape[2]
    tk = min(tk, seq)
    expand_t = (jnp.arange(seq)[:, None] // SEL_BLOCK == jnp.arange(nsel)[None, :]).astype(BF16)
    expand_t = expand_t.reshape(seq // tk, tk, nsel)
    qblk = pl.BlockSpec((1, tq, width), lambda b, i: (b, i, 0))
    kvblk = pl.BlockSpec((1, seq, LANES), lambda b, i: (b, 0, 0))
    return pl.pallas_call(
        functools.partial(_sattn_kernel, tk),
        grid=(bsz, seq // tq),
        in_specs=[qblk, kvblk, kvblk, kvblk, kvblk,
                  pl.BlockSpec((1, NSA_KV_GROUPS, nsel, tq), lambda b, i: (b, 0, 0, i)),
                  pl.BlockSpec((seq // tk, tk, nsel), lambda b, i: (0, 0, 0)),
                  pl.BlockSpec((1, width, tq), lambda b, i: (b, 0, i)),
                  pl.BlockSpec((1, tq, LANES), lambda b, i: (b, i, 0))],
        out_specs=qblk,
        out_shape=jax.ShapeDtypeStruct((bsz, seq, width), BF16),
        compiler_params=_cparams(("parallel", "parallel")),
        name="sattn",
    )(nq, ks, vs, kw, vw, sel_t, expand_t, oc_t, gate)


_NEXT = PEER_TOPK + 1
_VROWS = 24


def _extract_desc_ties(v, count, rows_out):
    nrow, tm = v.shape
    rid = lax.broadcasted_iota(jnp.int32, (nrow, tm), 0)
    oid = lax.broadcasted_iota(jnp.int32, (rows_out, tm), 0)
    out = jnp.full((rows_out, tm), -jnp.inf, F32)
    for k in range(count):
        m = jnp.max(v, axis=0, keepdims=True)
        idx = jnp.min(jnp.where(v == m, rid, nrow), axis=0, keepdims=True)
        v = jnp.where(rid == idx, -jnp.inf, v)
        out = jnp.where(oid == k, m, out)
    return out, jnp.zeros((1, tm), F32)


def _extract_desc_distinct(v, count, rows_out):
    _, tm = v.shape
    oid = lax.broadcasted_iota(jnp.int32, (rows_out, tm), 0)
    out = jnp.full((rows_out, tm), -jnp.inf, F32)
    masked_before = jnp.sum(jnp.where(v == -jnp.inf, 1.0, 0.0), axis=0, keepdims=True)
    for k in range(count):
        m = jnp.max(v, axis=0, keepdims=True)
        v = jnp.where(v == m, -jnp.inf, v)
        out = jnp.where(oid == k, m, out)
    masked_after = jnp.sum(jnp.where(v == -jnp.inf, 1.0, 0.0), axis=0, keepdims=True)
    return out, masked_after - masked_before - count


def _pq_kernel(ret_ref, nsa_ref, x_ref, ga_ref, sc_ref, sh_ref, g_ref, wo1_ref, wo2_ref, wq_ref, sk_ref,
               x1_ref, h2_ref, e1_ref, ph_ref, e0_ref):
    mix = _dot(ret_ref[0], wo1_ref[...]) + _dot(nsa_ref[0], wo2_ref[...])
    x1 = x_ref[0] + ga_ref[0] * mix
    x1_ref[0] = x1
    h2 = _norm_mod(x1, g_ref[...], sc_ref[0], sh_ref[0]).astype(BF16)
    h2_ref[0] = h2
    qp = _dot(h2, wq_ref[...]).astype(BF16)
    dh = sk_ref.shape[3]
    tm = qp.shape[0]
    rid8 = lax.broadcasted_iota(jnp.int32, (8, tm), 0)

    def routing(extract):
        extra = jnp.zeros((1, tm), F32)
        for p in range(PEER_HEADS):
            s0 = _dot_nt(sk_ref[p, 0], qp[:, (2 * p) * dh:(2 * p + 1) * dh])
            s1 = _dot_nt(sk_ref[p, 1], qp[:, (2 * p + 1) * dh:(2 * p + 2) * dh])
            v0, x0 = extract(s0, _NEXT, _VROWS)
            v1, x1_ = extract(s1, _NEXT, _VROWS)
            slabs = [v0[0:1] + v1]
            for a in range(1, 8):
                nb = _NEXT // (a + 1)
                slabs.append(jnp.where(rid8 < nb, v0[a:a + 1] + v1[0:8], -jnp.inf))
            slabs.append(v0[8:_VROWS] + v1[0:1])
            top, x2 = extract(jnp.concatenate(slabs, axis=0), _NEXT, _VROWS)
            extra = jnp.maximum(extra, jnp.maximum(jnp.maximum(x0, x1_), x2))
            mx = top[0:1]
            kid = lax.broadcasted_iota(jnp.int32, top.shape, 0)
            z = jnp.sum(jnp.where(kid < PEER_TOPK, jnp.exp(top - mx), 0.0), axis=0, keepdims=True)
            tau = 0.5 * (top[PEER_TOPK - 1:PEER_TOPK] + top[PEER_TOPK:PEER_TOPK + 1])
            e1_ref[p] = jnp.exp(s1 - v1[0:1])
            ph_ref[p] = jnp.exp(tau - s0 - v1[0:1])
            e0_ref[p] = jnp.exp(s0 - v0[0:1]) / z
        return extra

    extra = routing(_extract_desc_distinct)

    @pl.when(jnp.max(extra) > 0.0)
    def _():
        routing(_extract_desc_ties)


def _pq(ret, nsa, x, ga1, sc2, sh2, g, w_out, w_q, subkeys, tm=256):
    bsz, seq, d = x.shape
    tm = min(tm, seq)
    ntok = bsz * seq
    nkeys = subkeys.shape[2]
    nq = w_q.shape[1]
    wr = ret.shape[2]
    wo1 = w_out[:wr].astype(BF16)
    wo2 = w_out[wr:].astype(BF16)
    nt = seq // tm
    tokblk = lambda w: pl.BlockSpec((1, tm, w), lambda b, i: (b, i, 0))
    modblk = pl.BlockSpec((1, 1, d), lambda b, i: (b, 0, 0))
    c2 = lambda b, i: (0, 0)
    tblk = pl.BlockSpec((PEER_HEADS, nkeys, tm), lambda b, i: (0, 0, b * nt + i))
    tsh = jax.ShapeDtypeStruct((PEER_HEADS, nkeys, ntok), F32)
    return pl.pallas_call(
        _pq_kernel,
        grid=(bsz, nt),
        in_specs=[tokblk(wr), tokblk(nsa.shape[2]), tokblk(d), modblk, modblk, modblk,
                  pl.BlockSpec((1, d), c2),
                  pl.BlockSpec(wo1.shape, c2), pl.BlockSpec(wo2.shape, c2),
                  pl.BlockSpec((d, nq), c2, pipeline_mode=pl.Buffered(1)),
                  pl.BlockSpec(subkeys.shape, lambda b, i: (0, 0, 0, 0))],
        out_specs=[tokblk(d), tokblk(d), tblk, tblk, tblk],
        out_shape=[jax.ShapeDtypeStruct((bsz, seq, d), F32),
                   jax.ShapeDtypeStruct((bsz, seq, d), BF16),
                   tsh, tsh, tsh],
        compiler_params=_cparams(("parallel", "parallel")),
        name="pq",
    )(ret, nsa, x, ga1.reshape(bsz, 1, d), sc2.reshape(bsz, 1, d), sh2.reshape(bsz, 1, d), g.reshape(1, d),
      wo1, wo2, w_q.astype(BF16), subkeys.astype(BF16))


def _peer_kernel(h_ref, dn_ref, up_ref, e1_ref, ph_ref, e0_ref, o_ref, c_ref):
    j = pl.program_id(1)

    @pl.when(j == 0)
    def _():
        o_ref[...] = jnp.zeros_like(o_ref)

    nkeys = e1_ref.shape[1]
    te = dn_ref.shape[0]
    a_t = _dot_nt(dn_ref[...], h_ref[...])
    for ii in range(te // nkeys):
        rows = slice(ii * nkeys, (ii + 1) * nkeys)
        w = None
        for p in range(PEER_HEADS):
            e1 = e1_ref[p]
            contrib = jnp.where(e1 >= ph_ref[p, ii:ii + 1, :], e1, 0.0) * e0_ref[p, ii:ii + 1, :]
            w = contrib if w is None else w + contrib
        c_ref[rows, :] = (w * jax.nn.gelu(a_t[rows])).astype(BF16)
    o_ref[...] += _dot(up_ref[...], c_ref[...])


def _peer(h2, down, up_t, e1, ph, e0, tm=512, te=1024):
    ntok, d = h2.shape
    nexp = down.shape[0]
    nkeys = e1.shape[1]
    tm = min(tm, ntok)
    tblk = pl.BlockSpec((PEER_HEADS, nkeys, tm), lambda i, j: (0, 0, i))
    iblk = pl.BlockSpec((PEER_HEADS, te // nkeys, tm), lambda i, j: (0, j, i))
    return pl.pallas_call(
        _peer_kernel,
        grid=(ntok // tm, nexp // te),
        in_specs=[pl.BlockSpec((tm, d), lambda i, j: (i, 0)),
                  pl.BlockSpec((te, d), lambda i, j: (j, 0)),
                  pl.BlockSpec((d, te), lambda i, j: (0, j)),
                  tblk, iblk, iblk],
        out_specs=pl.BlockSpec((d, tm), lambda i, j: (0, i)),
        out_shape=jax.ShapeDtypeStruct((d, ntok), F32),
        scratch_shapes=[pltpu.VMEM((te, tm), BF16)],
        compiler_params=_cparams(("parallel", "arbitrary")),
        name="peer",
    )(h2, down, up_t, e1, ph, e0)


def _final_kernel(normalize, x_ref, p_ref, ga_ref, g_ref, o_ref):
    x = x_ref[0] + ga_ref[0] * p_ref[...].T
    if normalize:
        x = x * lax.rsqrt(jnp.mean(x * x, axis=-1, keepdims=True) + NORM_EPS) * g_ref[...]
    o_ref[0] = x


def _final(x1, peer_t, ga2, g, normalize, tm=256):
    bsz, seq, d = x1.shape
    tm = min(tm, seq)
    nt = seq // tm
    return pl.pallas_call(
        functools.partial(_final_kernel, normalize),
        grid=(bsz, nt),
        in_specs=[pl.BlockSpec((1, tm, d), lambda b, i: (b, i, 0)),
                  pl.BlockSpec((d, tm), lambda b, i: (0, b * nt + i)),
                  pl.BlockSpec((1, 1, d), lambda b, i: (b, 0, 0)),
                  pl.BlockSpec((1, d), lambda b, i: (0, 0))],
        out_specs=pl.BlockSpec((1, tm, d), lambda b, i: (b, i, 0)),
        out_shape=jax.ShapeDtypeStruct((bsz, seq, d), F32),
        compiler_params=_cparams(("parallel", "parallel")),
        name="final",
    )(x1, peer_t, ga2.reshape(bsz, 1, d), g.reshape(1, d))


def _layer(x, mod, g_mix, w_in, pe_k, pe_v, wk1, bk1, wk2, wv1, bv1, wv2, w_out, g_ffn, w_q, subkeys, down, up):
    bsz, seq, d = x.shape
    sh1, sc1, ga1, sh2, sc2, ga2 = jnp.split(mod, 6, axis=-1)
    pr = _inproj(x, sc1, sh1, g_mix, w_in)
    ret = _retention(pr["rq"], pr["rk"], pr["rv"], pr["rg"])
    kcmp, vcmp = _compress(pr["kc"], pr["vc"], pe_k, pe_v, wk1, bk1, wk2, wv1, bv1, wv2)
    oc, sel = _cattn(pr["nq"], kcmp, vcmp)
    nsa = _sattn(pr["nq"], pr["ks"], pr["vs"], pr["kw"], pr["vw"], sel, oc, pr["gate"])
    x1, h2, e1, ph, e0 = _pq(ret, nsa, x, ga1, sc2, sh2, g_ffn, w_out, w_q, subkeys)
    peer_t = _peer(h2.reshape(bsz * seq, d), down.astype(BF16), up.T.astype(BF16), e1, ph, e0)
    return x1, peer_t, ga2


def kernel(x, c, w_ada, b_ada, g_norm_mix, w_in, pe_cmp_k, pe_cmp_v, w_cmp_k1, b_cmp_k1, w_cmp_k2, w_cmp_v1,
           b_cmp_v1, w_cmp_v2, w_out, g_norm_ffn, w_peer_q, peer_subkeys, peer_down, peer_up, g_norm_final):
    depth = w_ada.shape[0]
    for l in range(depth):
        mod = _ada(c, w_ada[l], b_ada[l])
        x1, peer_t, ga2 = _layer(x, mod, g_norm_mix[l], w_in[l], pe_cmp_k[l], pe_cmp_v[l], w_cmp_k1[l],
                                 b_cmp_k1[l], w_cmp_k2[l], w_cmp_v1[l], b_cmp_v1[l], w_cmp_v2[l], w_out[l],
                                 g_norm_ffn[l], w_peer_q[l], peer_subkeys[l], peer_down[l], peer_up[l])
        x = _final(x1, peer_t, ga2, g_norm_final, normalize=(l == depth - 1))
    return x
```

```python
import functools
import math

import jax
import jax.numpy as jnp
from jax import lax
from jax.experimental import pallas as pl
from jax.experimental.pallas import tpu as pltpu

F32 = jnp.float32
BF16 = jnp.bfloat16

RET_HEADS = 4
RET_DK = 128
RET_DV = 128
RET_CHUNK = 128
NSA_HEADS = 8
NSA_KV_GROUPS = 2
NSA_HPG = NSA_HEADS // NSA_KV_GROUPS
NSA_DH = 64
CMP_LEN = 32
CMP_STRIDE = 16
CMP_HID = 256
SEL_BLOCK = 64
SEL_TOPN = 16
WINDOW = 512
FORCED_SCORE = 1e9
ROPE_THETA = 10000.0
PEER_HEADS = 8
PEER_TOPK = 16
NORM_EPS = 1e-6

LANES = 128
MASK_NEG = -1e30
VMEM_LIMIT = 56 * 1024 * 1024


def _cparams(sem):
    return pltpu.CompilerParams(dimension_semantics=sem, vmem_limit_bytes=VMEM_LIMIT)


def _dot(a, b):
    return jnp.dot(a, b, preferred_element_type=F32)


def _dot_nt(a, b):
    return lax.dot_general(a, b, (((1,), (1,)), ((), ())), preferred_element_type=F32)


def _dot_tn(a, b):
    return lax.dot_general(a, b, (((0,), (0,)), ((), ())), preferred_element_type=F32)


def _split_bf16(a):
    hi = a.astype(BF16)
    lo = (a - hi.astype(F32)).astype(BF16)
    return hi, lo


def _ada_kernel(c_ref, w_ref, b_ref, o_ref):
    c = c_ref[...]
    a = c * jax.nn.sigmoid(c)
    a_hi, a_lo = _split_bf16(a)
    w_hi, w_lo = _split_bf16(w_ref[...])
    o_ref[...] = _dot(a_hi, w_hi) + _dot(a_hi, w_lo) + _dot(a_lo, w_hi) + b_ref[...]


def _ada(c, w, b):
    bsz, d = c.shape
    n = w.shape[1]
    rows = 16
    cp = jnp.zeros((rows, d), F32).at[:bsz].set(c)
    tn = 1024
    out = pl.pallas_call(
        _ada_kernel,
        grid=(n // tn,),
        in_specs=[pl.BlockSpec((rows, d), lambda j: (0, 0)),
                  pl.BlockSpec((d, tn), lambda j: (0, j)),
                  pl.BlockSpec((1, tn), lambda j: (0, j))],
        out_specs=pl.BlockSpec((rows, tn), lambda j: (0, j)),
        out_shape=jax.ShapeDtypeStruct((rows, n), F32),
        compiler_params=_cparams(("parallel",)),
        name="ada",
    )(cp, w, b.reshape(1, n))
    return out[:bsz]


_ROPE_SEGS = (("rq", RET_HEADS * RET_DK, RET_DK, 1.0),
              ("rk", RET_HEADS * RET_DK, RET_DK, RET_DK ** -0.5),
              ("nq", NSA_HEADS * NSA_DH, NSA_DH, NSA_DH ** -0.5),
              ("kc", LANES, NSA_DH, 1.0),
              ("ks", LANES, NSA_DH, 1.0),
              ("kw", LANES, NSA_DH, 1.0))
_PLAIN_SEGS = (("rv", RET_HEADS * RET_DV, BF16),
               ("rg", RET_HEADS * RET_DV, F32),
               ("vc", LANES, BF16),
               ("vs", LANES, BF16),
               ("vw", LANES, BF16),
               ("gate", LANES, F32))


def _norm_mod(x, g, sc, sh):
    y = x * lax.rsqrt(jnp.mean(x * x, axis=-1, keepdims=True) + NORM_EPS)
    return (y * g) * (1.0 + sc) + sh


def _inproj_kernel(x_ref, sc_ref, sh_ref, g_ref, w_ref, wrot_ref, wpl_ref,
                   c128_ref, s128_ref, c64_ref, s64_ref, *out_refs):
    h = _norm_mod(x_ref[0], g_ref[...], sc_ref[0], sh_ref[0])
    hb = h.astype(BF16)
    outs = list(out_refs)
    off = 0
    for (_, width, hd, scale) in _ROPE_SEGS:
        o_ref = outs.pop(0)
        p = _dot(hb, w_ref[:, off:off + width])
        pr = _dot(hb, wrot_ref[:, off:off + width])
        cos = c128_ref[...] if hd == LANES else c64_ref[...]
        sin = s128_ref[...] if hd == LANES else s64_ref[...]
        for cb in range(width // LANES):
            sl = slice(cb * LANES, (cb + 1) * LANES)
            o = p[:, sl] * cos + pr[:, sl] * sin
            if scale != 1.0:
                o = o * scale
            o_ref[0, :, sl] = o.astype(o_ref.dtype)
        off += width
    off = 0
    for (_, width, _) in _PLAIN_SEGS:
        o_ref = outs.pop(0)
        o_ref[0] = _dot(hb, wpl_ref[:, off:off + width]).astype(o_ref.dtype)
        off += width


def _rot_cols(wseg, hd):
    d = wseg.shape[0]
    w3 = wseg.reshape(d, -1, hd)
    half = hd // 2
    return jnp.concatenate([-w3[..., half:], w3[..., :half]], axis=-1).reshape(d, -1)


def _rope_tables(seq, hd):
    pos = jnp.arange(seq)
    inv = ROPE_THETA ** (-jnp.arange(0, hd, 2, dtype=F32) / hd)
    ang = pos.astype(F32)[:, None] * inv[None, :]
    reps = 2 * LANES // hd
    return jnp.tile(jnp.cos(ang), (1, reps)), jnp.tile(jnp.sin(ang), (1, reps))


def _inproj(x, sc, sh, g, w_in, tm=512):
    bsz, seq, d = x.shape
    splits = (RET_HEADS * RET_DK, RET_HEADS * RET_DK, RET_HEADS * RET_DV, RET_HEADS * RET_DV,
              NSA_HEADS * NSA_DH) + (NSA_KV_GROUPS * NSA_DH,) * 6 + (NSA_HEADS * 3,)
    pts = []
    acc = 0
    for s in splits[:-1]:
        acc += s
        pts.append(acc)
    (r_q, r_k, r_v, r_g, n_q, n_kc, n_vc, n_ks, n_vs, n_kw, n_vw, n_gate) = jnp.split(w_in, pts, axis=-1)
    rope_w = {"rq": r_q, "rk": r_k, "nq": n_q, "kc": n_kc, "ks": n_ks, "kw": n_kw}
    w_rope = jnp.concatenate([rope_w[n] for (n, _, _, _) in _ROPE_SEGS], axis=-1).astype(BF16)
    w_rot = jnp.concatenate([_rot_cols(rope_w[n], hd) for (n, _, hd, _) in _ROPE_SEGS], axis=-1).astype(BF16)
    gate_pad = jnp.zeros((d, LANES), F32).at[:, :NSA_HEADS * 3].set(n_gate)
    w_plain = jnp.concatenate([r_v, r_g, n_vc, n_vs, n_vw, gate_pad], axis=-1).astype(BF16)
    c128, s128 = _rope_tables(seq, RET_DK)
    c64, s64 = _rope_tables(seq, NSA_DH)

    tm = min(tm, seq)
    nrope = w_rope.shape[1]
    npl = w_plain.shape[1]
    const2 = lambda b, i: (0, 0)
    tab = pl.BlockSpec((tm, LANES), lambda b, i: (i, 0))
    in_specs = [pl.BlockSpec((1, tm, d), lambda b, i: (b, i, 0)),
                pl.BlockSpec((1, 1, d), lambda b, i: (b, 0, 0)),
                pl.BlockSpec((1, 1, d), lambda b, i: (b, 0, 0)),
                pl.BlockSpec((1, d), const2),
                pl.BlockSpec((d, nrope), const2, pipeline_mode=pl.Buffered(1)),
                pl.BlockSpec((d, nrope), const2, pipeline_mode=pl.Buffered(1)),
                pl.BlockSpec((d, npl), const2, pipeline_mode=pl.Buffered(1)),
                tab, tab, tab, tab]
    out_specs = []
    out_shape = []
    for (_, width, _, _) in _ROPE_SEGS:
        out_specs.append(pl.BlockSpec((1, tm, width), lambda b, i: (b, i, 0)))
        out_shape.append(jax.ShapeDtypeStruct((bsz, seq, width), BF16))
    for (_, width, dt) in _PLAIN_SEGS:
        out_specs.append(pl.BlockSpec((1, tm, width), lambda b, i: (b, i, 0)))
        out_shape.append(jax.ShapeDtypeStruct((bsz, seq, width), dt))
    outs = pl.pallas_call(
        _inproj_kernel,
        grid=(bsz, seq // tm),
        in_specs=in_specs,
        out_specs=out_specs,
        out_shape=out_shape,
        compiler_params=_cparams(("parallel", "parallel")),
        name="inproj",
    )(x, sc.reshape(bsz, 1, d), sh.reshape(bsz, 1, d), g.reshape(1, d), w_rope, w_rot, w_plain,
      c128, s128, c64, s64)
    names = [n for (n, _, _, _) in _ROPE_SEGS] + [n for (n, _, _) in _PLAIN_SEGS]
    return dict(zip(names, outs))


def _ret_kernel(q_ref, k_ref, v_ref, g_ref, dmat_ref, xi_ref, zeta_ref, cd_ref, o_ref, state_ref):
    @pl.when(pl.program_id(1) == 0)
    def _():
        state_ref[...] = jnp.zeros_like(state_ref)

    for h in range(RET_HEADS):
        sl = slice(h * RET_DK, (h + 1) * RET_DK)
        q = q_ref[0, :, sl]
        k = k_ref[0, :, sl]
        v = v_ref[0, :, sl]
        state = state_ref[h]
        scores = _dot_nt(q, k) * dmat_ref[h]
        inner = _dot(scores.astype(BF16), v)
        cross = _dot(q, state.astype(BF16)) * xi_ref[h]
        y = inner + cross
        mu = jnp.mean(y, axis=-1, keepdims=True)
        yc = y - mu
        var = jnp.mean(yc * yc, axis=-1, keepdims=True)
        yn = yc * lax.rsqrt(var + NORM_EPS)
        g = g_ref[0, :, sl]
        o_ref[0, :, sl] = (g * jax.nn.sigmoid(g) * yn).astype(o_ref.dtype)
        kz = (k.astype(F32) * zeta_ref[h]).astype(BF16)
        state_ref[h] = cd_ref[h] * state + _dot_tn(kz, v)


def _retention(q, k, v, g):
    bsz, seq, _ = q.shape
    c = RET_CHUNK
    nh = RET_HEADS
    log_g = jnp.log1p(-jnp.exp2(-5.0 - jnp.arange(nh, dtype=F32)))
    n = jnp.arange(c, dtype=F32)
    diff = n[:, None] - n[None, :]
    causal = diff >= 0
    dmat = jnp.where(causal[None], jnp.exp(jnp.where(causal, diff, 0.0)[None] * log_g[:, None, None]), 0.0)
    zeta = jnp.exp((c - 1 - n)[None, :] * log_g[:, None])
    xi = jnp.exp((n + 1.0)[None, :] * log_g[:, None])
    cdecay = jnp.exp(c * log_g)
    xi_b = jnp.broadcast_to(xi[:, :, None], (nh, c, RET_DV))
    zeta_b = jnp.broadcast_to(zeta[:, :, None], (nh, c, RET_DK))
    cd_b = jnp.broadcast_to(cdecay[:, None, None], (nh, 1, RET_DV))
    width = nh * RET_DK
    blk = pl.BlockSpec((1, c, width), lambda b, i: (b, i, 0))
    const3 = lambda b, i: (0, 0, 0)
    return pl.pallas_call(
        _ret_kernel,
        grid=(bsz, seq // c),
        in_specs=[blk, blk, blk, blk,
                  pl.BlockSpec((nh, c, c), const3),
                  pl.BlockSpec((nh, c, RET_DV), const3),
                  pl.BlockSpec((nh, c, RET_DK), const3),
                  pl.BlockSpec((nh, 1, RET_DV), const3)],
        out_specs=blk,
        out_shape=jax.ShapeDtypeStruct((bsz, seq, width), BF16),
        scratch_shapes=[pltpu.VMEM((nh, RET_DK, RET_DV), F32)],
        compiler_params=_cparams(("parallel", "arbitrary")),
        name="ret",
    )(q, k, v, g, dmat, xi_b, zeta_b, cd_b)


def _cmp_kernel(ncb, xk_ref, xv_ref, wk1_ref, wv1_ref, pk_ref, pv_ref, bk_ref, bv_ref, wk2_ref, wv2_ref,
                ko_ref, vo_ref, shift_ref):
    nrow = xk_ref.shape[1]
    row = lax.broadcasted_iota(jnp.int32, (nrow, 1), 0)
    for (x_ref, w1_ref, p_ref, b_ref, w2_ref, o_ref) in (
            (xk_ref, wk1_ref, pk_ref, bk_ref, wk2_ref, ko_ref),
            (xv_ref, wv1_ref, pv_ref, bv_ref, wv2_ref, vo_ref)):
        a = _dot(x_ref[0], w1_ref[...])
        pw = _dot(p_ref[...], w1_ref[...])
        for g in range(NSA_KV_GROUPS):
            c0 = g * 2 * CMP_HID
            lo = a[:, c0:c0 + CMP_HID]
            hi = a[:, c0 + CMP_HID:c0 + 2 * CMP_HID]
            shift_ref[pl.ds(0, nrow), :] = hi
            shift_ref[pl.ds(nrow, 8), :] = jnp.zeros((8, CMP_HID), F32)
            hi_next = shift_ref[pl.ds(1, nrow), :]
            bias = pw[0:1, c0:c0 + CMP_HID] + pw[1:2, c0 + CMP_HID:c0 + 2 * CMP_HID] + b_ref[...]
            hid = jax.nn.gelu(lo + hi_next + bias)
            out = _dot(hid.astype(BF16), w2_ref[...])
            o_ref[0, g] = jnp.where(row < ncb, out, 0.0).astype(o_ref.dtype)


def _cmp_weights(w1, pe):
    hid = w1.shape[1]
    w1r = w1.reshape(2, CMP_STRIDE, NSA_DH, hid)
    blocks = []
    for g in range(NSA_KV_GROUPS):
        for half in range(2):
            blk = jnp.zeros((CMP_STRIDE, NSA_KV_GROUPS, NSA_DH, hid), F32).at[:, g].set(w1r[half])
            blocks.append(blk.reshape(CMP_STRIDE * NSA_KV_GROUPS * NSA_DH, hid))
    wbig = jnp.concatenate(blocks, axis=-1).astype(BF16)
    per = pe.reshape(2, CMP_STRIDE, 1, NSA_DH)
    pch = jnp.broadcast_to(per, (2, CMP_STRIDE, NSA_KV_GROUPS, NSA_DH)).reshape(2, -1)
    prow = jnp.zeros((16, pch.shape[1]), F32).at[:2].set(pch).astype(BF16)
    return wbig, prow


def _compress(kraw, vraw, pe_k, pe_v, wk1, bk1, wk2, wv1, bv1, wv2):
    bsz, seq, width = kraw.shape
    nrow = seq // CMP_STRIDE
    ncb = (seq - CMP_LEN) // CMP_STRIDE + 1
    xk = kraw.reshape(bsz, nrow, CMP_STRIDE * width)
    xv = vraw.reshape(bsz, nrow, CMP_STRIDE * width)
    wkb, pk = _cmp_weights(wk1, pe_k)
    wvb, pv = _cmp_weights(wv1, pe_v)
    xblk = pl.BlockSpec((1, nrow, CMP_STRIDE * width), lambda b: (b, 0, 0))
    c2 = lambda b: (0, 0)
    oblk = pl.BlockSpec((1, NSA_KV_GROUPS, nrow, NSA_DH), lambda b: (b, 0, 0, 0))
    osh = jax.ShapeDtypeStruct((bsz, NSA_KV_GROUPS, nrow, NSA_DH), BF16)
    return pl.pallas_call(
        functools.partial(_cmp_kernel, ncb),
        grid=(bsz,),
        in_specs=[xblk, xblk,
                  pl.BlockSpec(wkb.shape, c2), pl.BlockSpec(wvb.shape, c2),
                  pl.BlockSpec(pk.shape, c2), pl.BlockSpec(pv.shape, c2),
                  pl.BlockSpec((1, CMP_HID), c2), pl.BlockSpec((1, CMP_HID), c2),
                  pl.BlockSpec((CMP_HID, NSA_DH), c2), pl.BlockSpec((CMP_HID, NSA_DH), c2)],
        out_specs=[oblk, oblk],
        out_shape=[osh, osh],
        scratch_shapes=[pltpu.VMEM((nrow + 8, CMP_HID), F32)],
        compiler_params=_cparams(("parallel",)),
        name="cmp",
    )(xk, xv, wkb, wvb, pk, pv, bk1.reshape(1, -1), bv1.reshape(1, -1), wk2.astype(BF16), wv2.astype(BF16))


def _softmax_cols(s, mask):
    s = jnp.where(mask, s, -jnp.inf)
    m = jnp.max(s, axis=0, keepdims=True)
    m = jnp.where(m > -jnp.inf, m, 0.0)
    e = jnp.where(mask, jnp.exp(s - m), 0.0)
    return e / jnp.maximum(jnp.sum(e, axis=0, keepdims=True), jnp.finfo(F32).tiny)


def _cattn_kernel(q_ref, kc_ref, vc_ref, ovt_ref, oct_ref, selt_ref):
    tq = q_ref.shape[1]
    nrow = kc_ref.shape[2]
    nsel = selt_ref.shape[2]
    n_top = min(SEL_TOPN, nsel)
    q0 = pl.program_id(1) * tq
    t = q0 + lax.broadcasted_iota(jnp.int32, (1, tq), 1)
    cmp_end = lax.broadcasted_iota(jnp.int32, (nrow, 1), 0) * CMP_STRIDE + (CMP_LEN - 1)
    cmask = cmp_end <= t
    blk = lax.broadcasted_iota(jnp.int32, (nsel, tq), 0)
    cur = t // SEL_BLOCK
    forced = (blk == 0) | (blk == cur) | (blk == cur - 1)
    for g in range(NSA_KV_GROUPS):
        kc = kc_ref[0, g]
        vc = vc_ref[0, g]
        imp = jnp.zeros((nsel, tq), F32)
        for h in range(NSA_HPG):
            hq = g * NSA_HPG + h
            sl = slice(hq * NSA_DH, (hq + 1) * NSA_DH)
            p = _softmax_cols(_dot_nt(kc, q_ref[0, :, sl]), cmask)
            pb = p.astype(BF16)
            oct_ref[0, sl, :] = _dot_tn(vc, pb)
            imp = imp + _dot(ovt_ref[...], pb)
        v = jnp.where(forced, FORCED_SCORE, imp)
        v = jnp.where(blk <= cur, v, -jnp.inf)
        sel = jnp.zeros((nsel, tq), F32)
        for _ in range(n_top):
            m = jnp.max(v, axis=0, keepdims=True)
            cand = (v == m) & (m > -jnp.inf)
            idx = jnp.min(jnp.where(cand, blk, nsel), axis=0, keepdims=True)
            pick = blk == idx
            sel = jnp.where(pick, 1.0, sel)
            v = jnp.where(pick, -jnp.inf, v)
        selt_ref[0, g] = sel.astype(selt_ref.dtype)


def _cattn(nq, kcmp, vcmp, tq=256):
    bsz, seq, width = nq.shape
    nrow = kcmp.shape[2]
    nsel = seq // SEL_BLOCK
    tq = min(tq, seq)
    cmp_start = jnp.arange(nrow) * CMP_STRIDE
    sel_start = jnp.arange(nsel) * SEL_BLOCK
    overlap = jnp.clip(jnp.minimum(cmp_start[:, None] + CMP_LEN, sel_start[None, :] + SEL_BLOCK)
                       - jnp.maximum(cmp_start[:, None], sel_start[None, :]), 0).astype(F32) / CMP_STRIDE
    kv = pl.BlockSpec((1, NSA_KV_GROUPS, nrow, NSA_DH), lambda b, i: (b, 0, 0, 0))
    return pl.pallas_call(
        _cattn_kernel,
        grid=(bsz, seq // tq),
        in_specs=[pl.BlockSpec((1, tq, width), lambda b, i: (b, i, 0)), kv, kv,
                  pl.BlockSpec((nsel, nrow), lambda b, i: (0, 0))],
        out_specs=[pl.BlockSpec((1, width, tq), lambda b, i: (b, 0, i)),
                   pl.BlockSpec((1, NSA_KV_GROUPS, nsel, tq), lambda b, i: (b, 0, 0, i))],
        out_shape=[jax.ShapeDtypeStruct((bsz, width, seq), F32),
                   jax.ShapeDtypeStruct((bsz, NSA_KV_GROUPS, nsel, seq), BF16)],
        compiler_params=_cparams(("parallel", "parallel")),
        name="cattn",
    )(nq, kcmp, vcmp, overlap.T.astype(BF16))


def _sattn_kernel(tk, q_ref, ks_ref, vs_ref, kw_ref, vw_ref, selt_ref, ext_ref, oct_ref, gate_ref, o_ref):
    tq = q_ref.shape[1]
    seq = ks_ref.shape[1]
    q0 = pl.program_id(1) * tq
    t = q0 + lax.broadcasted_iota(jnp.int32, (1, tq), 1)
    n_kt = (q0 + tq - 1) // tk + 1
    wlen = min(WINDOW + tq, seq)
    w0 = pl.multiple_of(jnp.maximum(q0 + tq - wlen, 0), 8)
    wpos = w0 + lax.broadcasted_iota(jnp.int32, (wlen, 1), 0)
    wbias = jnp.where((wpos <= t) & (wpos > t - WINDOW), 0.0, MASK_NEG)
    wbias = jnp.concatenate([wbias] * NSA_HPG, axis=1)
    gate_t = jax.nn.sigmoid(gate_ref[0]).T
    qs_g = [jnp.concatenate(
        [q_ref[0, :, (g * NSA_HPG + h) * NSA_DH:(g * NSA_HPG + h + 1) * NSA_DH] for h in range(NSA_HPG)],
        axis=0) for g in range(NSA_KV_GROUPS)]

    def body(kt, carry):
        k0 = pl.multiple_of(kt * tk, tk)
        kpos = k0 + lax.broadcasted_iota(jnp.int32, (tk, 1), 0)
        causal = kpos <= t
        new = []
        for g in range(NSA_KV_GROUPS):
            m, l, acc = carry[g]
            gs = slice(g * NSA_DH, (g + 1) * NSA_DH)
            kblk = ks_ref[0, pl.ds(k0, tk), gs]
            vblk = vs_ref[0, pl.ds(k0, tk), gs]
            allowed = (_dot(ext_ref[kt], selt_ref[0, g]) > 0.5) & causal
            bias = jnp.where(allowed, 0.0, MASK_NEG)
            s = _dot_nt(kblk, qs_g[g]) + jnp.concatenate([bias] * NSA_HPG, axis=1)
            m_new = jnp.maximum(m, jnp.max(s, axis=0, keepdims=True))
            alpha = jnp.exp(m - m_new)
            p = jnp.exp(s - m_new)
            l_new = alpha * l + jnp.sum(p, axis=0, keepdims=True)
            new.append((m_new, l_new, alpha * acc + _dot_tn(vblk, p.astype(BF16))))
        return tuple(new)

    init = tuple((jnp.full((1, NSA_HPG * tq), MASK_NEG, F32),
                  jnp.zeros((1, NSA_HPG * tq), F32),
                  jnp.zeros((NSA_DH, NSA_HPG * tq), F32)) for _ in range(NSA_KV_GROUPS))
    sel_state = lax.fori_loop(0, n_kt, body, init)

    outs = []
    for g in range(NSA_KV_GROUPS):
        gs = slice(g * NSA_DH, (g + 1) * NSA_DH)
        qs = qs_g[g]
        _, l, acc = sel_state[g]
        o_s = acc / l

        kwb = kw_ref[0, pl.ds(w0, wlen), gs]
        vwb = vw_ref[0, pl.ds(w0, wlen), gs]
        sw = _dot_nt(kwb, qs) + wbias
        mw = jnp.max(sw, axis=0, keepdims=True)
        pw = jnp.exp(sw - mw)
        lw = jnp.sum(pw, axis=0, keepdims=True)
        o_w = _dot_tn(vwb, pw.astype(BF16)) / lw

        for h in range(NSA_HPG):
            hq = g * NSA_HPG + h
            cs = slice(h * tq, (h + 1) * tq)
            outs.append(gate_t[3 * hq:3 * hq + 1] * oct_ref[0, hq * NSA_DH:(hq + 1) * NSA_DH, :]
                        + gate_t[3 * hq + 1:3 * hq + 2] * o_s[:, cs]
                        + gate_t[3 * hq + 2:3 * hq + 3] * o_w[:, cs])
    o_ref[0] = jnp.concatenate(outs, axis=0).T.astype(o_ref.dtype)


def _sattn(nq, ks, vs, kw, vw, sel_t, oc_t, gate, tq=128, tk=512):
    bsz, seq, width = nq.shape
    nsel = sel_t.shape[2]
    tk = min(tk, seq)
    expand_t = (jnp.arange(seq)[:, None] // SEL_BLOCK == jnp.arange(nsel)[None, :]).astype(BF16)
    expand_t = expand_t.reshape(seq // tk, tk, nsel)
    qblk = pl.BlockSpec((1, tq, width), lambda b, i: (b, i, 0))
    kvblk = pl.BlockSpec((1, seq, LANES), lambda b, i: (b, 0, 0))
    return pl.pallas_call(
        functools.partial(_sattn_kernel, tk),
        grid=(bsz, seq // tq),
        in_specs=[qblk, kvblk, kvblk, kvblk, kvblk,
                  pl.BlockSpec((1, NSA_KV_GROUPS, nsel, tq), lambda b, i: (b, 0, 0, i)),
                  pl.BlockSpec((seq // tk, tk, nsel), lambda b, i: (0, 0, 0)),
                  pl.BlockSpec((1, width, tq), lambda b, i: (b, 0, i)),
                  pl.BlockSpec((1, tq, LANES), lambda b, i: (b, i, 0))],
        out_specs=qblk,
        out_shape=jax.ShapeDtypeStruct((bsz, seq, width), BF16),
        compiler_params=_cparams(("parallel", "parallel")),
        name="sattn",
    )(nq, ks, vs, kw, vw, sel_t, expand_t, oc_t, gate)


_NEXT = PEER_TOPK + 1
_VROWS = 24


def _extract_desc_ties(v, count, rows_out):
    nrow, tm = v.shape
    rid = lax.broadcasted_iota(jnp.int32, (nrow, tm), 0)
    oid = lax.broadcasted_iota(jnp.int32, (rows_out, tm), 0)
    out = jnp.full((rows_out, tm), -jnp.inf, F32)
    for k in range(count):
        m = jnp.max(v, axis=0, keepdims=True)
        idx = jnp.min(jnp.where(v == m, rid, nrow), axis=0, keepdims=True)
        v = jnp.where(rid == idx, -jnp.inf, v)
        out = jnp.where(oid == k, m, out)
    return out, jnp.zeros((1, tm), F32)


def _extract_desc_distinct(v, count, rows_out):
    _, tm = v.shape
    oid = lax.broadcasted_iota(jnp.int32, (rows_out, tm), 0)
    out = jnp.full((rows_out, tm), -jnp.inf, F32)
    masked_before = jnp.sum(jnp.where(v == -jnp.inf, 1.0, 0.0), axis=0, keepdims=True)
    for k in range(count):
        m = jnp.max(v, axis=0, keepdims=True)
        v = jnp.where(v == m, -jnp.inf, v)
        out = jnp.where(oid == k, m, out)
    masked_after = jnp.sum(jnp.where(v == -jnp.inf, 1.0, 0.0), axis=0, keepdims=True)
    return out, masked_after - masked_before - count


def _pq_kernel(ret_ref, nsa_ref, x_ref, ga_ref, sc_ref, sh_ref, g_ref, wo1_ref, wo2_ref, wq_ref, sk_ref,
               x1_ref, h2_ref, e1_ref, ph_ref, e0_ref):
    mix = _dot(ret_ref[0], wo1_ref[...]) + _dot(nsa_ref[0], wo2_ref[...])
    x1 = x_ref[0] + ga_ref[0] * mix
    x1_ref[0] = x1
    h2 = _norm_mod(x1, g_ref[...], sc_ref[0], sh_ref[0]).astype(BF16)
    h2_ref[0] = h2
    qp = _dot(h2, wq_ref[...]).astype(BF16)
    dh = sk_ref.shape[3]
    tm = qp.shape[0]
    rid8 = lax.broadcasted_iota(jnp.int32, (8, tm), 0)

    def routing(extract):
        extra = jnp.zeros((1, tm), F32)
        for p in range(PEER_HEADS):
            s0 = _dot_nt(sk_ref[p, 0], qp[:, (2 * p) * dh:(2 * p + 1) * dh])
            s1 = _dot_nt(sk_ref[p, 1], qp[:, (2 * p + 1) * dh:(2 * p + 2) * dh])
            v0, x0 = extract(s0, _NEXT, _VROWS)
            v1, x1_ = extract(s1, _NEXT, _VROWS)
            slabs = [v0[0:1] + v1]
            for a in range(1, 8):
                nb = _NEXT // (a + 1)
                slabs.append(jnp.where(rid8 < nb, v0[a:a + 1] + v1[0:8], -jnp.inf))
            slabs.append(v0[8:_VROWS] + v1[0:1])
            top, x2 = extract(jnp.concatenate(slabs, axis=0), _NEXT, _VROWS)
            extra = jnp.maximum(extra, jnp.maximum(jnp.maximum(x0, x1_), x2))
            mx = top[0:1]
            kid = lax.broadcasted_iota(jnp.int32, top.shape, 0)
            z = jnp.sum(jnp.where(kid < PEER_TOPK, jnp.exp(top - mx), 0.0), axis=0, keepdims=True)
            tau = 0.5 * (top[PEER_TOPK - 1:PEER_TOPK] + top[PEER_TOPK:PEER_TOPK + 1])
            e1_ref[p] = jnp.exp(s1 - v1[0:1])
            ph_ref[p] = jnp.exp(tau - s0 - v1[0:1])
            e0_ref[p] = jnp.exp(s0 - v0[0:1]) / z
        return extra

    extra = routing(_extract_desc_distinct)

    @pl.when(jnp.max(extra) > 0.0)
    def _():
        routing(_extract_desc_ties)


def _pq(ret, nsa, x, ga1, sc2, sh2, g, w_out, w_q, subkeys, tm=512):
    bsz, seq, d = x.shape
    tm = min(tm, seq)
    ntok = bsz * seq
    nkeys = subkeys.shape[2]
    nq = w_q.shape[1]
    wr = ret.shape[2]
    wo1 = w_out[:wr].astype(BF16)
    wo2 = w_out[wr:].astype(BF16)
    nt = seq // tm
    tokblk = lambda w: pl.BlockSpec((1, tm, w), lambda b, i: (b, i, 0))
    modblk = pl.BlockSpec((1, 1, d), lambda b, i: (b, 0, 0))
    c2 = lambda b, i: (0, 0)
    tblk = pl.BlockSpec((PEER_HEADS, nkeys, tm), lambda b, i: (0, 0, b * nt + i))
    tsh = jax.ShapeDtypeStruct((PEER_HEADS, nkeys, ntok), F32)
    return pl.pallas_call(
        _pq_kernel,
        grid=(bsz, nt),
        in_specs=[tokblk(wr), tokblk(nsa.shape[2]), tokblk(d), modblk, modblk, modblk,
                  pl.BlockSpec((1, d), c2),
                  pl.BlockSpec(wo1.shape, c2), pl.BlockSpec(wo2.shape, c2),
                  pl.BlockSpec((d, nq), c2, pipeline_mode=pl.Buffered(1)),
                  pl.BlockSpec(subkeys.shape, lambda b, i: (0, 0, 0, 0))],
        out_specs=[tokblk(d), tokblk(d), tblk, tblk, tblk],
        out_shape=[jax.ShapeDtypeStruct((bsz, seq, d), F32),
                   jax.ShapeDtypeStruct((bsz, seq, d), BF16),
                   tsh, tsh, tsh],
        compiler_params=_cparams(("parallel", "parallel")),
        name="pq",
    )(ret, nsa, x, ga1.reshape(bsz, 1, d), sc2.reshape(bsz, 1, d), sh2.reshape(bsz, 1, d), g.reshape(1, d),
      wo1, wo2, w_q.astype(BF16), subkeys.astype(BF16))


def _peer_kernel(h_ref, dn_ref, up_ref, e1_ref, ph_ref, e0_ref, o_ref, c_ref):
    j = pl.program_id(1)

    @pl.when(j == 0)
    def _():
        o_ref[...] = jnp.zeros_like(o_ref)

    nkeys = e1_ref.shape[1]
    te = dn_ref.shape[0]
    a_t = _dot_nt(dn_ref[...], h_ref[...])
    for ii in range(te // nkeys):
        rows = slice(ii * nkeys, (ii + 1) * nkeys)
        w = None
        for p in range(PEER_HEADS):
            e1 = e1_ref[p]
            contrib = jnp.where(e1 >= ph_ref[p, ii:ii + 1, :], e1, 0.0) * e0_ref[p, ii:ii + 1, :]
            w = contrib if w is None else w + contrib
        c_ref[rows, :] = (w * jax.nn.gelu(a_t[rows])).astype(BF16)
    o_ref[...] += _dot(up_ref[...], c_ref[...])


def _peer(h2, down, up_t, e1, ph, e0, tm=512, te=1024):
    ntok, d = h2.shape
    nexp = down.shape[0]
    nkeys = e1.shape[1]
    tm = min(tm, ntok)
    tblk = pl.BlockSpec((PEER_HEADS, nkeys, tm), lambda i, j: (0, 0, i))
    iblk = pl.BlockSpec((PEER_HEADS, te // nkeys, tm), lambda i, j: (0, j, i))
    return pl.pallas_call(
        _peer_kernel,
        grid=(ntok // tm, nexp // te),
        in_specs=[pl.BlockSpec((tm, d), lambda i, j: (i, 0)),
                  pl.BlockSpec((te, d), lambda i, j: (j, 0)),
                  pl.BlockSpec((d, te), lambda i, j: (0, j)),
                  tblk, iblk, iblk],
        out_specs=pl.BlockSpec((d, tm), lambda i, j: (0, i)),
        out_shape=jax.ShapeDtypeStruct((d, ntok), F32),
        scratch_shapes=[pltpu.VMEM((te, tm), BF16)],
        compiler_params=_cparams(("parallel", "arbitrary")),
        name="peer",
    )(h2, down, up_t, e1, ph, e0)


def _final_kernel(normalize, x_ref, p_ref, ga_ref, g_ref, o_ref):
    x = x_ref[0] + ga_ref[0] * p_ref[...].T
    if normalize:
        x = x * lax.rsqrt(jnp.mean(x * x, axis=-1, keepdims=True) + NORM_EPS) * g_ref[...]
    o_ref[0] = x


def _final(x1, peer_t, ga2, g, normalize, tm=256):
    bsz, seq, d = x1.shape
    tm = min(tm, seq)
    nt = seq // tm
    return pl.pallas_call(
        functools.partial(_final_kernel, normalize),
        grid=(bsz, nt),
        in_specs=[pl.BlockSpec((1, tm, d), lambda b, i: (b, i, 0)),
                  pl.BlockSpec((d, tm), lambda b, i: (0, b * nt + i)),
                  pl.BlockSpec((1, 1, d), lambda b, i: (b, 0, 0)),
                  pl.BlockSpec((1, d), lambda b, i: (0, 0))],
        out_specs=pl.BlockSpec((1, tm, d), lambda b, i: (b, i, 0)),
        out_shape=jax.ShapeDtypeStruct((bsz, seq, d), F32),
        compiler_params=_cparams(("parallel", "parallel")),
        name="final",
    )(x1, peer_t, ga2.reshape(bsz, 1, d), g.reshape(1, d))


def _layer(x, mod, g_mix, w_in, pe_k, pe_v, wk1, bk1, wk2, wv1, bv1, wv2, w_out, g_ffn, w_q, subkeys, down, up):
    bsz, seq, d = x.shape
    sh1, sc1, ga1, sh2, sc2, ga2 = jnp.split(mod, 6, axis=-1)
    pr = _inproj(x, sc1, sh1, g_mix, w_in)
    ret = _retention(pr["rq"], pr["rk"], pr["rv"], pr["rg"])
    kcmp, vcmp = _compress(pr["kc"], pr["vc"], pe_k, pe_v, wk1, bk1, wk2, wv1, bv1, wv2)
    oc, sel = _cattn(pr["nq"], kcmp, vcmp)
    nsa = _sattn(pr["nq"], pr["ks"], pr["vs"], pr["kw"], pr["vw"], sel, oc, pr["gate"])
    x1, h2, e1, ph, e0 = _pq(ret, nsa, x, ga1, sc2, sh2, g_ffn, w_out, w_q, subkeys)
    peer_t = _peer(h2.reshape(bsz * seq, d), down.astype(BF16), up.T.astype(BF16), e1, ph, e0)
    return x1, peer_t, ga2


def kernel(x, c, w_ada, b_ada, g_norm_mix, w_in, pe_cmp_k, pe_cmp_v, w_cmp_k1, b_cmp_k1, w_cmp_k2, w_cmp_v1,
           b_cmp_v1, w_cmp_v2, w_out, g_norm_ffn, w_peer_q, peer_subkeys, peer_down, peer_up, g_norm_final):
    depth = w_ada.shape[0]
    for l in range(depth):
        mod = _ada(c, w_ada[l], b_ada[l])
        x1, peer_t, ga2 = _layer(x, mod, g_norm_mix[l], w_in[l], pe_cmp_k[l], pe_cmp_v[l], w_cmp_k1[l],
                                 b_cmp_k1[l], w_cmp_k2[l], w_cmp_v1[l], b_cmp_v1[l], w_cmp_v2[l], w_out[l],
                                 g_norm_ffn[l], w_peer_q[l], peer_subkeys[l], peer_down[l], peer_up[l])
        x = _final(x1, peer_t, ga2, g_norm_final, normalize=(l == depth - 1))
    return x
```
